```python
import jax, jax.numpy as jnp
from jax import lax
import numpy as np

D_MODEL = 2048
BATCH = 2
SEQ = 16384
DEPTH = 2

N_META = 16
CONV_WIDTH = D_MODEL
CONV_KERNEL = 31
RET_HEADS = 8
RET_QK_DIM = D_MODEL // RET_HEADS
RET_V_DIM = 2 * D_MODEL // RET_HEADS
RET_QK_WIDTH = RET_HEADS * RET_QK_DIM
RET_WIDTH = RET_HEADS * RET_V_DIM
CHUNK = 128
ROPE_BASE = 10000.0
EPS = 1e-6

OFF_GLU_A = 0
OFF_GLU_B = OFF_GLU_A + CONV_WIDTH
OFF_CONV_GATE = OFF_GLU_B + CONV_WIDTH
OFF_Q = OFF_CONV_GATE + CONV_WIDTH
OFF_K = OFF_Q + RET_QK_WIDTH
OFF_V = OFF_K + RET_QK_WIDTH
OFF_RET_GATE = OFF_V + RET_WIDTH
OFF_MERGE_A = OFF_RET_GATE + RET_WIDTH
OFF_MERGE_B = OFF_MERGE_A + D_MODEL
IN_COLS = OFF_MERGE_B + D_MODEL

kernel_name = "hybrid_conformer_retention_gated"


def rmsnorm(x, g):
    xf = x.astype(jnp.float32)
    y = xf * lax.rsqrt(jnp.mean(xf * xf, axis=-1, keepdims=True) + EPS)
    return (y * g.astype(jnp.float32)).astype(x.dtype)


def layernorm(x, w, b):
    xf = x.astype(jnp.float32)
    mu = jnp.mean(xf, axis=-1, keepdims=True)
    var = jnp.mean(jnp.square(xf - mu), axis=-1, keepdims=True)
    y = (xf - mu) * lax.rsqrt(var + EPS)
    return (y * w.astype(jnp.float32) + b.astype(jnp.float32)).astype(x.dtype)


def causal_depthwise_conv(u, w, b):
    out = lax.conv_general_dilated(
        u, w[:, None, :].astype(u.dtype), window_strides=(1,),
        padding=[(CONV_KERNEL - 1, 0)],
        dimension_numbers=("NWC", "WIO", "NWC"),
        feature_group_count=u.shape[-1])
    return out + b.astype(u.dtype)


def conformer_branch(a, gl, gate, conv_w, conv_b, ln_w, ln_b, w_proj):
    u = a * jax.nn.sigmoid(gl)
    u = causal_depthwise_conv(u, conv_w, conv_b)
    u = layernorm(u, ln_w, ln_b)
    u = jax.nn.silu(u) * jax.nn.silu(gate)
    return u @ w_proj


def rotate_pairs(x, cos, sin):
    xr = x.reshape(x.shape[:-1] + (x.shape[-1] // 2, 2))
    x0, x1 = xr[..., 0], xr[..., 1]
    out = jnp.stack([x0 * cos - x1 * sin, x1 * cos + x0 * sin], axis=-1)
    return out.reshape(x.shape)


def retention_chunkwise(q, k, v):
    b, L, h, dk = q.shape
    dv = v.shape[-1]
    pad = (-L) % CHUNK
    padw = ((0, 0), (pad, 0), (0, 0), (0, 0))
    q, k, v = jnp.pad(q, padw), jnp.pad(k, padw), jnp.pad(v, padw)
    n = (L + pad) // CHUNK

    def to_chunks(t):
        return t.reshape(b, n, CHUNK, h, t.shape[-1]).transpose(1, 0, 3, 2, 4)

    qs, ks, vs = to_chunks(q), to_chunks(k), to_chunks(v)

    log_g = jnp.log(1.0 - jnp.exp2(-5.0 - jnp.arange(h, dtype=jnp.float32)))
    idx = jnp.arange(CHUNK, dtype=jnp.float32)
    dist = idx[:, None] - idx[None, :]
    decay_in = jnp.where(dist[None] >= 0,
                         jnp.exp(log_g[:, None, None] * jnp.maximum(dist, 0.0)[None]),
                         0.0)
    xi = jnp.exp(log_g[:, None] * (idx + 1.0)[None])
    zeta = jnp.exp(log_g[:, None] * (CHUNK - 1.0 - idx)[None])
    g_chunk = jnp.exp(log_g * CHUNK)

    def step(state, xs):
        qc, kc, vc = xs
        s = jnp.einsum("bhid,bhjd->bhij", qc, kc) * decay_in[None]
        inner = jnp.einsum("bhij,bhjv->bhiv", s, vc)
        cross = jnp.einsum("bhid,bhdv->bhiv", qc, state) * xi[None, :, :, None]
        state = state * g_chunk[None, :, None, None] + jnp.einsum(
            "bhjd,bhjv->bhdv", kc * zeta[None, :, :, None], vc)
        return state, inner + cross

    state0 = jnp.zeros((b, h, dk, dv), jnp.float32)
    _, o = lax.scan(step, state0, (qs, ks, vs))
    o = o.transpose(1, 0, 3, 2, 4).reshape(b, n * CHUNK, h, dv)
    return o[:, pad:]


def head_groupnorm(o, w, b):
    mu = jnp.mean(o, axis=-1, keepdims=True)
    var = jnp.mean(jnp.square(o - mu), axis=-1, keepdims=True)
    y = ((o - mu) * lax.rsqrt(var + EPS)).reshape(o.shape[0], o.shape[1], -1)
    return y * w.astype(jnp.float32) + b.astype(jnp.float32)


def setup_inputs(seed: int = 0) -> dict:
    key = jax.random.key(seed)
    ks = jax.random.split(key, 16)
    f32 = jnp.float32
    nrm = lambda k, shape, scale: (jax.random.normal(k, shape, f32) * scale)
    return {
        "x": nrm(ks[0], (BATCH, SEQ, D_MODEL), 1.0),
        "meta_tokens": nrm(ks[1], (N_META, D_MODEL), 1.0),
        "norm_g": 1.0 + nrm(ks[2], (DEPTH, D_MODEL), 0.02),
        "w_in": nrm(ks[3], (DEPTH, D_MODEL, IN_COLS), D_MODEL ** -0.5),
        "conv_w": nrm(ks[4], (DEPTH, CONV_KERNEL, CONV_WIDTH), CONV_KERNEL ** -0.5),
        "conv_b": nrm(ks[5], (DEPTH, CONV_WIDTH), 0.02),
        "conv_ln_w": 1.0 + nrm(ks[6], (DEPTH, CONV_WIDTH), 0.02),
        "conv_ln_b": nrm(ks[7], (DEPTH, CONV_WIDTH), 0.02),
        "conv_out": nrm(ks[8], (DEPTH, CONV_WIDTH, D_MODEL), CONV_WIDTH ** -0.5),
        "ret_gn_w": 1.0 + nrm(ks[9], (DEPTH, RET_WIDTH), 0.02),
        "ret_gn_b": nrm(ks[10], (DEPTH, RET_WIDTH), 0.02),
        "ret_out": nrm(ks[11], (DEPTH, RET_WIDTH, D_MODEL), RET_WIDTH ** -0.5),
        "w_out": nrm(ks[12], (DEPTH, D_MODEL, D_MODEL), D_MODEL ** -0.5),
        "final_norm_g": 1.0 + nrm(ks[13], (D_MODEL,), 0.02),
    }


def reference(x, meta_tokens, norm_g, w_in, conv_w, conv_b, conv_ln_w, conv_ln_b,
              conv_out, ret_gn_w, ret_gn_b, ret_out, w_out, final_norm_g):
    b = x.shape[0]
    meta = jnp.broadcast_to(meta_tokens[None].astype(x.dtype), (b, N_META, D_MODEL))
    h_res = jnp.concatenate([meta, x], axis=1)
    L = h_res.shape[1]

    pos = jnp.arange(L, dtype=jnp.float32)
    inv_freq = 1.0 / (ROPE_BASE ** jnp.linspace(0.0, 1.0, RET_QK_DIM // 2, dtype=jnp.float32))
    ang = pos[:, None] * inv_freq[None]
    cos = jnp.cos(ang)[None, :, None, :]
    sin = jnp.sin(ang)[None, :, None, :]

    for l in range(DEPTH):
        hn = rmsnorm(h_res, norm_g[l])
        p = hn @ w_in[l]
        glu_a = p[..., OFF_GLU_A:OFF_GLU_B]
        glu_b = p[..., OFF_GLU_B:OFF_CONV_GATE]
        conv_gate = p[..., OFF_CONV_GATE:OFF_Q]
        q = p[..., OFF_Q:OFF_K].reshape(b, L, RET_HEADS, RET_QK_DIM)
        k = p[..., OFF_K:OFF_V].reshape(b, L, RET_HEADS, RET_QK_DIM)
        v = p[..., OFF_V:OFF_RET_GATE].reshape(b, L, RET_HEADS, RET_V_DIM)
        ret_gate = p[..., OFF_RET_GATE:OFF_MERGE_A]
        merge_a = p[..., OFF_MERGE_A:OFF_MERGE_B]
        merge_b = p[..., OFF_MERGE_B:IN_COLS]

        y_a = conformer_branch(glu_a, glu_b, conv_gate, conv_w[l], conv_b[l],
                               conv_ln_w[l], conv_ln_b[l], conv_out[l])

        qf = rotate_pairs(q.astype(jnp.float32), cos, sin) * (RET_QK_DIM ** -0.5)
        kf = rotate_pairs(k.astype(jnp.float32), cos, sin)
        o = retention_chunkwise(qf, kf, v.astype(jnp.float32))
        o = head_groupnorm(o, ret_gn_w[l], ret_gn_b[l]).astype(h_res.dtype)
        y_b = (o * jax.nn.silu(ret_gate)) @ ret_out[l]

        merged = jax.nn.sigmoid(merge_a) * y_a + jax.nn.sigmoid(merge_b) * y_b
        h_res = h_res + merged @ w_out[l]

    y = rmsnorm(h_res, final_norm_g)
    return y[:, N_META:]
```

```python
import functools

import jax
import jax.numpy as jnp
from jax import lax
from jax.experimental import pallas as pl
from jax.experimental.pallas import tpu as pltpu

F32 = jnp.float32
BF16 = jnp.bfloat16

LANES = 128
ROW_TILE = 256
FRONT_ROWS = ROW_TILE
CONV_HIST = 32
CONV_ROW_BLOCK = 64
RET_HEADS = 8
ROPE_BASE = 10000.0
EPS = 1e-6
VMEM_LIMIT = 56 * 1024 * 1024


def _silu(v):
    return v * jax.nn.sigmoid(v)


def _params(sem, vmem=VMEM_LIMIT):
    return pltpu.CompilerParams(dimension_semantics=sem, vmem_limit_bytes=vmem)


def _rmsnorm_kernel(x_ref, g_ref, o_ref):
    x = x_ref[...]
    y = x * lax.rsqrt(jnp.mean(x * x, axis=-1, keepdims=True) + EPS)
    o_ref[...] = (y * g_ref[...]).astype(o_ref.dtype)


def _rmsnorm(x2d, g, out_dtype):
    m, d = x2d.shape
    tm = ROW_TILE
    return pl.pallas_call(
        _rmsnorm_kernel,
        out_shape=jax.ShapeDtypeStruct((m, d), out_dtype),
        grid=(m // tm,),
        in_specs=[pl.BlockSpec((tm, d), lambda i: (i, 0)),
                  pl.BlockSpec((1, d), lambda i: (0, 0))],
        out_specs=pl.BlockSpec((tm, d), lambda i: (i, 0)),
        compiler_params=_params(("parallel",)),
        name="rmsnorm",
    )(x2d, g.reshape(1, d))


def _matmul_kernel(x_ref, w_ref, o_ref):
    o_ref[...] = jnp.dot(x_ref[...], w_ref[...],
                         preferred_element_type=F32).astype(o_ref.dtype)


def _in_proj(hn, w, tm, tn):
    m, k = hn.shape
    n = w.shape[1]
    return pl.pallas_call(
        _matmul_kernel,
        out_shape=jax.ShapeDtypeStruct((m, n), BF16),
        grid=(m // tm, n // tn),
        in_specs=[pl.BlockSpec((tm, k), lambda i, j: (i, 0)),
                  pl.BlockSpec((k, tn), lambda i, j: (0, j))],
        out_specs=pl.BlockSpec((tm, tn), lambda i, j: (i, j)),
        compiler_params=_params(("parallel", "arbitrary")),
        name="in_proj",
    )(hn, w)


def _conformer_kernel(a_ref, gl_ref, gate_ref, ma_ref, cw_ref, cb_ref, lnw_ref, lnb_ref,
                      wproj_ref, o_ref, uext_ref, conv_ref, *, kw):
    t = pl.program_id(1)
    tt = a_ref.shape[0]
    d = a_ref.shape[1]
    ncb = d // LANES
    first = CONV_HIST - (kw - 1)

    @pl.when(t == 0)
    def _():
        uext_ref[:, 0:CONV_HIST, :] = jnp.zeros((ncb, CONV_HIST, LANES), F32)

    @pl.when(t > 0)
    def _():
        uext_ref[:, 0:CONV_HIST, :] = uext_ref[:, tt:tt + CONV_HIST, :]

    u = a_ref[...].astype(F32) * jax.nn.sigmoid(gl_ref[...].astype(F32))
    for c in range(ncb):
        uext_ref[c, CONV_HIST:CONV_HIST + tt, :] = u[:, c * LANES:(c + 1) * LANES]

    def conv_block(c, carry):
        for rb in range(tt // CONV_ROW_BLOCK):
            r0 = rb * CONV_ROW_BLOCK
            acc = jnp.broadcast_to(cb_ref[c], (CONV_ROW_BLOCK, LANES))
            for k in range(kw):
                acc = acc + cw_ref[c, k:k + 1, :] * uext_ref[c, r0 + first + k:r0 + first + k + CONV_ROW_BLOCK, :]
            conv_ref[c, r0:r0 + CONV_ROW_BLOCK, :] = acc
        return carry

    lax.fori_loop(0, ncb, conv_block, 0)

    cv = conv_ref[...]
    mu = jnp.sum(jnp.sum(cv, axis=0), axis=-1, keepdims=True) / d
    dev = cv - mu[None]
    var = jnp.sum(jnp.sum(dev * dev, axis=0), axis=-1, keepdims=True) / d
    y = dev * lax.rsqrt(var + EPS)[None] * lnw_ref[...] + lnb_ref[...]
    act = _silu(y)
    gate = gate_ref[...].astype(F32)
    lhs = jnp.concatenate(
        [(act[c] * _silu(gate[:, c * LANES:(c + 1) * LANES])).astype(BF16) for c in range(ncb)],
        axis=1)
    ya = jnp.dot(lhs, wproj_ref[...], preferred_element_type=F32)
    o_ref[...] = jax.nn.sigmoid(ma_ref[...].astype(F32)) * ya


def _conformer(p, conv_w, conv_b, ln_w, ln_b, w_proj, nb, nt, cols):
    m = p.shape[0]
    d = w_proj.shape[1]
    kw = conv_w.shape[0]
    ncb = d // LANES
    tt = ROW_TILE
    cw = conv_w.reshape(kw, ncb, LANES).transpose(1, 0, 2)
    blk = lambda col: pl.BlockSpec((tt, d), lambda b, t, col=col: (b * nt + t, col))
    vec3 = pl.BlockSpec((ncb, 1, LANES), lambda b, t: (0, 0, 0))
    return pl.pallas_call(
        functools.partial(_conformer_kernel, kw=kw),
        out_shape=jax.ShapeDtypeStruct((m, d), F32),
        grid=(nb, nt),
        in_specs=[blk(cols["glu_a"]), blk(cols["glu_b"]), blk(cols["conv_gate"]), blk(cols["merge_a"]),
                  pl.BlockSpec((ncb, kw, LANES), lambda b, t: (0, 0, 0)),
                  vec3, vec3, vec3,
                  pl.BlockSpec((d, d), lambda b, t: (0, 0), pipeline_mode=pl.Buffered(1))],
        out_specs=pl.BlockSpec((tt, d), lambda b, t: (b * nt + t, 0)),
        scratch_shapes=[pltpu.VMEM((ncb, CONV_HIST + tt, LANES), F32),
                        pltpu.VMEM((ncb, tt, LANES), F32)],
        compiler_params=_params(("parallel", "arbitrary")),
        name="conformer",
    )(p, p, p, p, cw, conv_b.reshape(ncb, 1, LANES), ln_w.reshape(ncb, 1, LANES),
      ln_b.reshape(ncb, 1, LANES), w_proj)


def _retention_kernel(logg_ref, q_ref, k_ref, v_ref, g_ref, cos_ref, sin_ref, gnw_ref, gnb_ref,
                      o_ref, state_ref, decay_ref, xi_ref, zeta_ref, *, chunk, scale):
    h = pl.program_id(1)
    t = pl.program_id(2)
    nsub = q_ref.shape[0] // chunk
    half = q_ref.shape[1] // 2
    lg = logg_ref[h]

    @pl.when(t == 0)
    def _():
        state_ref[...] = jnp.zeros(state_ref.shape, F32)
        i = lax.broadcasted_iota(jnp.int32, (chunk, chunk), 0)
        j = lax.broadcasted_iota(jnp.int32, (chunk, chunk), 1)
        dist = (i - j).astype(F32)
        decay_ref[...] = jnp.where(dist >= 0, jnp.exp(lg * jnp.maximum(dist, 0.0)), 0.0)
        idx = lax.broadcasted_iota(jnp.int32, (chunk, 1), 0).astype(F32)
        xi_ref[...] = jnp.exp(lg * (idx + 1.0))
        zeta_ref[...] = jnp.exp(lg * (chunk - 1.0 - idx))

    g_chunk = jnp.exp(jnp.full((1, 1), lg * chunk, F32))

    def rotate(x, cos, sin):
        xe, xo = x[:, :half], x[:, half:]
        return jnp.concatenate([xe * cos - xo * sin, xo * cos + xe * sin], axis=1)

    def step(c, carry):
        rows = pl.ds(pl.multiple_of(c * chunk, chunk), chunk)
        cos = cos_ref[rows, :]
        sin = sin_ref[rows, :]
        qr = rotate(q_ref[rows, :].astype(F32), cos, sin) * scale
        kr = rotate(k_ref[rows, :].astype(F32), cos, sin)
        v = v_ref[rows, :]
        state = state_ref[...]
        qb = qr.astype(BF16)
        s = lax.dot_general(qb, kr.astype(BF16), (((1,), (1,)), ((), ())),
                            preferred_element_type=F32) * decay_ref[...]
        inner = jnp.dot(s.astype(BF16), v, preferred_element_type=F32)
        cross = jnp.dot(qb, state.astype(BF16), preferred_element_type=F32) * xi_ref[...]
        kz = (kr * zeta_ref[...]).astype(BF16)
        state_ref[...] = state * g_chunk + lax.dot_general(
            kz, v, (((0,), (0,)), ((), ())), preferred_element_type=F32)
        o = inner + cross
        mu = jnp.mean(o, axis=-1, keepdims=True)
        dev = o - mu
        var = jnp.mean(dev * dev, axis=-1, keepdims=True)
        y = dev * lax.rsqrt(var + EPS) * gnw_ref[...] + gnb_ref[...]
        o_ref[rows, :] = (y * _silu(g_ref[rows, :].astype(F32))).astype(o_ref.dtype)
        return carry

    lax.fori_loop(0, nsub, step, 0)


def _retention(p, cos, sin, log_g, gn_w, gn_b, nb, nt, tr, cols, dk, dv):
    m = p.shape[0]
    heads = RET_HEADS
    chunk = ROW_TILE
    qblk = lambda col: pl.BlockSpec((tr, dk), lambda b, h, t, col=col: (b * nt + t, col + h))
    vblk = lambda col: pl.BlockSpec((tr, dv), lambda b, h, t, col=col: (b * nt + t, col + h))
    tab = pl.BlockSpec((tr, dk // 2), lambda b, h, t: (t, 0))
    vec = pl.BlockSpec((1, dv), lambda b, h, t: (0, h))
    return pl.pallas_call(
        functools.partial(_retention_kernel, chunk=chunk, scale=dk ** -0.5),
        out_shape=jax.ShapeDtypeStruct((m, heads * dv), BF16),
        grid=(nb, heads, nt),
        in_specs=[pl.BlockSpec(memory_space=pltpu.SMEM),
                  qblk(cols["q"] // dk), qblk(cols["k"] // dk),
                  vblk(cols["v"] // dv), vblk(cols["ret_gate"] // dv),
                  tab, tab, vec, vec],
        out_specs=pl.BlockSpec((tr, dv), lambda b, h, t: (b * nt + t, h)),
        scratch_shapes=[pltpu.VMEM((dk, dv), F32),
                        pltpu.VMEM((chunk, chunk), F32),
                        pltpu.VMEM((chunk, 1), F32),
                        pltpu.VMEM((chunk, 1), F32)],
        compiler_params=_params(("parallel", "parallel", "arbitrary")),
        name="retention",
    )(log_g, p, p, p, p, cos, sin, gn_w.reshape(1, -1), gn_b.reshape(1, -1))


def _merge_kernel(act_ref, w_ref, mb_ref, ma_ref, o_ref):
    yb = jnp.dot(act_ref[...], w_ref[...], preferred_element_type=F32)
    o_ref[...] = (ma_ref[...] + jax.nn.sigmoid(mb_ref[...].astype(F32)) * yb).astype(o_ref.dtype)


def _merge(act_b, w_ret, p, m_a, col_mb):
    m, kdim = act_b.shape
    d = w_ret.shape[1]
    tm = ROW_TILE
    return pl.pallas_call(
        _merge_kernel,
        out_shape=jax.ShapeDtypeStruct((m, d), BF16),
        grid=(m // tm,),
        in_specs=[pl.BlockSpec((tm, kdim), lambda i: (i, 0)),
                  pl.BlockSpec((kdim, d), lambda i: (0, 0), pipeline_mode=pl.Buffered(1)),
                  pl.BlockSpec((tm, d), lambda i: (i, col_mb)),
                  pl.BlockSpec((tm, d), lambda i: (i, 0))],
        out_specs=pl.BlockSpec((tm, d), lambda i: (i, 0)),
        compiler_params=_params(("parallel",)),
        name="merge",
    )(act_b, w_ret, p, m_a)


def _out_kernel(mg_ref, w_ref, h_ref, g_ref, hout_ref, nout_ref, *, nt, pad_rows):
    i = pl.program_id(0)
    tm = h_ref.shape[0]
    hn = h_ref[...] + jnp.dot(mg_ref[...], w_ref[...], preferred_element_type=F32)
    row = (i % nt) * tm + lax.broadcasted_iota(jnp.int32, (tm, 1), 0)
    hn = jnp.where(row >= pad_rows, hn, 0.0)
    hout_ref[...] = hn
    y = hn * lax.rsqrt(jnp.mean(hn * hn, axis=-1, keepdims=True) + EPS)
    nout_ref[...] = (y * g_ref[...]).astype(nout_ref.dtype)


def _out_proj(merged, w_out, h, g_next, nt, pad_rows, norm_dtype):
    m, d = h.shape
    tm = ROW_TILE
    row = pl.BlockSpec((tm, d), lambda i: (i, 0))
    return pl.pallas_call(
        functools.partial(_out_kernel, nt=nt, pad_rows=pad_rows),
        out_shape=(jax.ShapeDtypeStruct((m, d), F32), jax.ShapeDtypeStruct((m, d), norm_dtype)),
        grid=(m // tm,),
        in_specs=[row,
                  pl.BlockSpec((d, d), lambda i: (0, 0), pipeline_mode=pl.Buffered(1)),
                  row,
                  pl.BlockSpec((1, d), lambda i: (0, 0))],
        out_specs=(row, row),
        compiler_params=_params(("parallel",)),
        name="out_proj",
    )(merged, w_out, h, g_next.reshape(1, d))


def _deinterleave_heads(w, heads):
    d, n = w.shape
    hd = n // heads
    return w.reshape(d, heads, hd // 2, 2).transpose(0, 1, 3, 2).reshape(d, n)


def kernel(x, meta_tokens, norm_g, w_in, conv_w, conv_b, conv_ln_w, conv_ln_b, conv_out,
           ret_gn_w, ret_gn_b, ret_out, w_out, final_norm_g):
    nb, seq, d = x.shape
    depth = w_in.shape[0]
    n_meta = meta_tokens.shape[0]
    heads = RET_HEADS
    conv_width = conv_out.shape[1]
    ret_width = ret_out.shape[1]
    dk = d // heads
    dv = ret_width // heads
    qk_width = heads * dk
    assert seq % ROW_TILE == 0 and n_meta <= FRONT_ROWS and conv_width == d

    cols = {}
    off = 0
    for name, width in (("glu_a", conv_width), ("glu_b", conv_width), ("conv_gate", conv_width),
                        ("q", qk_width), ("k", qk_width), ("v", ret_width), ("ret_gate", ret_width),
                        ("merge_a", d), ("merge_b", d)):
        cols[name] = off
        off += width
    assert off == w_in.shape[2]
    wide = {n: cols[n] // d for n in ("glu_a", "glu_b", "conv_gate", "merge_a", "merge_b")}

    pad_rows = FRONT_ROWS - n_meta
    lp = FRONT_ROWS + seq
    nt = lp // ROW_TILE
    m = nb * lp

    front = jnp.concatenate([jnp.zeros((pad_rows, d), x.dtype), meta_tokens.astype(x.dtype)], axis=0)
    h = jnp.concatenate([jnp.broadcast_to(front[None], (nb, FRONT_ROWS, d)), x], axis=1).reshape(m, d)

    pos = (jnp.arange(lp, dtype=jnp.int32) - pad_rows).astype(F32)
    inv_freq = 1.0 / (ROPE_BASE ** jnp.linspace(0.0, 1.0, dk // 2, dtype=F32))
    ang = pos[:, None] * inv_freq[None]
    cos, sin = jnp.cos(ang), jnp.sin(ang)
    log_g = jnp.log(1.0 - jnp.exp2(-5.0 - jnp.arange(heads, dtype=F32)))

    tm_proj = 1280 if m % 1280 == 0 else ROW_TILE
    tr = 1280 if lp % 1280 == 0 else ROW_TILE

    hn = _rmsnorm(h, norm_g[0], BF16)
    for l in range(depth):
        wl = w_in[l]
        w_l = jnp.concatenate(
            [wl[:, :cols["q"]],
             _deinterleave_heads(wl[:, cols["q"]:cols["k"]], heads),
             _deinterleave_heads(wl[:, cols["k"]:cols["v"]], heads),
             wl[:, cols["v"]:]], axis=1).astype(BF16)
        p = _in_proj(hn, w_l, tm_proj, 1024)
        m_a = _conformer(p, conv_w[l], conv_b[l], conv_ln_w[l], conv_ln_b[l],
                         conv_out[l].astype(BF16), nb, nt, wide)
        act_b = _retention(p, cos, sin, log_g, ret_gn_w[l], ret_gn_b[l], nb, lp // tr, tr, cols, dk, dv)
        merged = _merge(act_b, ret_out[l].astype(BF16), p, m_a, wide["merge_b"])
        last = l == depth - 1
        g_next = final_norm_g if last else norm_g[l + 1]
        h, hn = _out_proj(merged, w_out[l].astype(BF16), h, g_next, nt, pad_rows,
                          x.dtype if last else BF16)
    return hn.reshape(nb, lp, d)[:, FRONT_ROWS:]
```

```python
import functools

import jax
import jax.numpy as jnp
from jax import lax
from jax.experimental import pallas as pl
from jax.experimental.pallas import tpu as pltpu

F32 = jnp.float32
BF16 = jnp.bfloat16

LANES = 128
ROW_TILE = 256
FRONT_ROWS = ROW_TILE
CONV_HIST = 32
CONV_ROW_BLOCK = 64
RET_HEADS = 8
ROPE_BASE = 10000.0
EPS = 1e-6
VMEM_LIMIT = 56 * 1024 * 1024


def _silu(v):
    return v * jax.nn.sigmoid(v)


def _params(sem, vmem=VMEM_LIMIT):
    return pltpu.CompilerParams(dimension_semantics=sem, vmem_limit_bytes=vmem)


def _front_or_x(t, front_ref, x_ref):
    return jnp.where(t == 0, front_ref[...], x_ref[...])


def _x_tile_spec(tm, d, nt):
    return pl.BlockSpec((None, tm, d), lambda i: (i // nt, jnp.maximum(i % nt - 1, 0), 0))


def _rmsnorm_kernel(front_ref, x_ref, g_ref, o_ref, *, nt):
    x = _front_or_x(pl.program_id(0) % nt, front_ref, x_ref)
    y = x * lax.rsqrt(jnp.mean(x * x, axis=-1, keepdims=True) + EPS)
    o_ref[...] = (y * g_ref[...]).astype(o_ref.dtype)


def _first_rmsnorm(front, x, g, nt):
    nb, _, d = x.shape
    tm = ROW_TILE
    return pl.pallas_call(
        functools.partial(_rmsnorm_kernel, nt=nt),
        out_shape=jax.ShapeDtypeStruct((nb * nt * tm, d), BF16),
        grid=(nb * nt,),
        in_specs=[pl.BlockSpec((tm, d), lambda i: (0, 0)),
                  _x_tile_spec(tm, d, nt),
                  pl.BlockSpec((1, d), lambda i: (0, 0))],
        out_specs=pl.BlockSpec((tm, d), lambda i: (i, 0)),
        compiler_params=_params(("arbitrary",)),
        name="rmsnorm",
    )(front, x, g.reshape(1, d))


def _matmul_kernel(x_ref, w_ref, o_ref):
    o_ref[...] = jnp.dot(x_ref[...], w_ref[...].astype(BF16),
                         preferred_element_type=F32).astype(o_ref.dtype)


def _in_proj(hn, w_all, layer, tm, tn):
    m, k = hn.shape
    n = w_all.shape[2]
    return pl.pallas_call(
        _matmul_kernel,
        out_shape=jax.ShapeDtypeStruct((m, n), BF16),
        grid=(m // tm, n // tn),
        in_specs=[pl.BlockSpec((tm, k), lambda i, j: (i, 0)),
                  pl.BlockSpec((None, k, tn), lambda i, j: (layer, 0, j))],
        out_specs=pl.BlockSpec((tm, tn), lambda i, j: (i, j)),
        compiler_params=_params(("parallel", "arbitrary")),
        name="in_proj",
    )(hn, w_all)


def _conformer_kernel(a_ref, gl_ref, gate_ref, ma_ref, cw_ref, cb_ref, lnw_ref, lnb_ref,
                      wproj_ref, o_ref, uext_ref, conv_ref, *, kw):
    t = pl.program_id(1)
    tt = a_ref.shape[0]
    d = a_ref.shape[1]
    ncb = d // LANES
    first = CONV_HIST - (kw - 1)

    @pl.when(t == 0)
    def _():
        uext_ref[:, 0:CONV_HIST, :] = jnp.zeros((ncb, CONV_HIST, LANES), F32)

    @pl.when(t > 0)
    def _():
        uext_ref[:, 0:CONV_HIST, :] = uext_ref[:, tt:tt + CONV_HIST, :]

    u = a_ref[...].astype(F32) * jax.nn.sigmoid(gl_ref[...].astype(F32))
    for c in range(ncb):
        uext_ref[c, CONV_HIST:CONV_HIST + tt, :] = u[:, c * LANES:(c + 1) * LANES]

    def conv_block(c, carry):
        for rb in range(tt // CONV_ROW_BLOCK):
            r0 = rb * CONV_ROW_BLOCK
            acc = jnp.broadcast_to(cb_ref[c], (CONV_ROW_BLOCK, LANES))
            for k in range(kw):
                acc = acc + cw_ref[c, k:k + 1, :] * uext_ref[c, r0 + first + k:r0 + first + k + CONV_ROW_BLOCK, :]
            conv_ref[c, r0:r0 + CONV_ROW_BLOCK, :] = acc
        return carry

    lax.fori_loop(0, ncb, conv_block, 0)

    cv = conv_ref[...]
    mu = jnp.sum(jnp.sum(cv, axis=0), axis=-1, keepdims=True) / d
    dev = cv - mu[None]
    var = jnp.sum(jnp.sum(dev * dev, axis=0), axis=-1, keepdims=True) / d
    y = dev * lax.rsqrt(var + EPS)[None] * lnw_ref[...] + lnb_ref[...]
    act = _silu(y)
    gate = gate_ref[...].astype(F32)
    lhs = jnp.concatenate(
        [(act[c] * _silu(gate[:, c * LANES:(c + 1) * LANES])).astype(BF16) for c in range(ncb)],
        axis=1)
    ya = jnp.dot(lhs, wproj_ref[...], preferred_element_type=F32)
    o_ref[...] = jax.nn.sigmoid(ma_ref[...].astype(F32)) * ya


def _conformer(p, conv_w, conv_b, ln_w, ln_b, w_proj, nb, nt, cols):
    m = p.shape[0]
    d = w_proj.shape[1]
    kw = conv_w.shape[0]
    ncb = d // LANES
    tt = ROW_TILE
    cw = conv_w.reshape(kw, ncb, LANES).transpose(1, 0, 2)
    blk = lambda col: pl.BlockSpec((tt, d), lambda b, t, col=col: (b * nt + t, col))
    vec3 = pl.BlockSpec((ncb, 1, LANES), lambda b, t: (0, 0, 0))
    return pl.pallas_call(
        functools.partial(_conformer_kernel, kw=kw),
        out_shape=jax.ShapeDtypeStruct((m, d), F32),
        grid=(nb, nt),
        in_specs=[blk(cols["glu_a"]), blk(cols["glu_b"]), blk(cols["conv_gate"]), blk(cols["merge_a"]),
                  pl.BlockSpec((ncb, kw, LANES), lambda b, t: (0, 0, 0)),
                  vec3, vec3, vec3,
                  pl.BlockSpec((d, d), lambda b, t: (0, 0), pipeline_mode=pl.Buffered(1))],
        out_specs=pl.BlockSpec((tt, d), lambda b, t: (b * nt + t, 0)),
        scratch_shapes=[pltpu.VMEM((ncb, CONV_HIST + tt, LANES), F32),
                        pltpu.VMEM((ncb, tt, LANES), F32)],
        compiler_params=_params(("parallel", "arbitrary")),
        name="conformer",
    )(p, p, p, p, cw, conv_b.reshape(ncb, 1, LANES), ln_w.reshape(ncb, 1, LANES),
      ln_b.reshape(ncb, 1, LANES), w_proj)


def _retention_kernel(logg_ref, q_ref, k_ref, v_ref, g_ref, cos_ref, sin_ref, gnw_ref, gnb_ref,
                      o_ref, state_ref, decay_ref, xi_ref, zeta_ref, *, chunk, scale):
    t = pl.program_id(0)
    b = pl.program_id(1)
    h = pl.program_id(2)
    nsub = q_ref.shape[0] // chunk
    dk = q_ref.shape[1]
    lg = logg_ref[h]
    slot = b * pl.num_programs(2) + h

    @pl.when(jnp.logical_and(t == 0, b == 0))
    def _():
        i = lax.broadcasted_iota(jnp.int32, (chunk, chunk), 0)
        j = lax.broadcasted_iota(jnp.int32, (chunk, chunk), 1)
        dist = (i - j).astype(F32)
        decay_ref[h] = jnp.where(dist >= 0, jnp.exp(lg * jnp.maximum(dist, 0.0)), 0.0)
        idx = lax.broadcasted_iota(jnp.int32, (chunk, 1), 0).astype(F32)
        xi_ref[h] = jnp.exp(lg * (idx + 1.0))
        zeta_ref[h] = jnp.exp(lg * (chunk - 1.0 - idx))

    @pl.when(t == 0)
    def _():
        state_ref[slot] = jnp.zeros(state_ref.shape[1:], F32)

    g_chunk = jnp.exp(jnp.full((1, 1), lg * chunk, F32))
    even = lax.broadcasted_iota(jnp.int32, (chunk, LANES), 1) % 2 == 0

    def rotate(x, cos2, sin2):
        parts = []
        for l0 in range(0, dk, LANES):
            xl = x[:, l0:l0 + LANES]
            nxt = pltpu.roll(xl, LANES - 1, 1)
            prv = pltpu.roll(xl, 1, 1)
            parts.append(xl * cos2[:, l0:l0 + LANES] + jnp.where(even, nxt, prv) * sin2[:, l0:l0 + LANES])
        return jnp.concatenate(parts, axis=1)

    def step(c, carry):
        rows = pl.ds(pl.multiple_of(c * chunk, chunk), chunk)
        cos2 = cos_ref[rows, :]
        sin2 = sin_ref[rows, :]
        qr = rotate(q_ref[rows, :].astype(F32), cos2, sin2) * scale
        kr = rotate(k_ref[rows, :].astype(F32), cos2, sin2)
        v = v_ref[rows, :]
        state = state_ref[slot]
        qb = qr.astype(BF16)
        s = lax.dot_general(qb, kr.astype(BF16), (((1,), (1,)), ((), ())),
                            preferred_element_type=F32) * decay_ref[h]
        inner = jnp.dot(s.astype(BF16), v, preferred_element_type=F32)
        cross = jnp.dot(qb, state.astype(BF16), preferred_element_type=F32) * xi_ref[h]
        kz = (kr * zeta_ref[h]).astype(BF16)
        state_ref[slot] = state * g_chunk + lax.dot_general(
            kz, v, (((0,), (0,)), ((), ())), preferred_element_type=F32)
        o = inner + cross
        mu = jnp.mean(o, axis=-1, keepdims=True)
        dev = o - mu
        var = jnp.mean(dev * dev, axis=-1, keepdims=True)
        y = dev * lax.rsqrt(var + EPS) * gnw_ref[...] + gnb_ref[...]
        o_ref[rows, :] = (y * _silu(g_ref[rows, :].astype(F32))).astype(o_ref.dtype)
        return carry

    lax.fori_loop(0, nsub, step, 0)


def _retention(p, cos2, sin2, log_g, gn_w, gn_b, nb, nt, tr, cols, dk, dv):
    m = p.shape[0]
    heads = RET_HEADS
    chunk = ROW_TILE
    qblk = lambda col: pl.BlockSpec((tr, dk), lambda t, b, h, col=col: (b * nt + t, col + h))
    vblk = lambda col: pl.BlockSpec((tr, dv), lambda t, b, h, col=col: (b * nt + t, col + h))
    tab = pl.BlockSpec((tr, dk), lambda t, b, h: (t, 0))
    vec = pl.BlockSpec((1, dv), lambda t, b, h: (0, h))
    return pl.pallas_call(
        functools.partial(_retention_kernel, chunk=chunk, scale=dk ** -0.5),
        out_shape=jax.ShapeDtypeStruct((m, heads * dv), BF16),
        grid=(nt, nb, heads),
        in_specs=[pl.BlockSpec(memory_space=pltpu.SMEM),
                  qblk(cols["q"] // dk), qblk(cols["k"] // dk),
                  vblk(cols["v"] // dv), vblk(cols["ret_gate"] // dv),
                  tab, tab, vec, vec],
        out_specs=pl.BlockSpec((tr, dv), lambda t, b, h: (b * nt + t, h)),
        scratch_shapes=[pltpu.VMEM((nb * heads, dk, dv), F32),
                        pltpu.VMEM((heads, chunk, chunk), F32),
                        pltpu.VMEM((heads, chunk, 1), F32),
                        pltpu.VMEM((heads, chunk, 1), F32)],
        compiler_params=_params(("arbitrary", "arbitrary", "arbitrary")),
        name="retention",
    )(log_g, p, p, p, p, cos2, sin2, gn_w.reshape(1, -1), gn_b.reshape(1, -1))


def _merge_kernel(act_ref, w_ref, mb_ref, ma_ref, o_ref):
    yb = jnp.dot(act_ref[...], w_ref[...], preferred_element_type=F32)
    o_ref[...] = (ma_ref[...] + jax.nn.sigmoid(mb_ref[...].astype(F32)) * yb).astype(o_ref.dtype)


def _merge(act_b, w_ret, p, m_a, col_mb):
    m, kdim = act_b.shape
    d = w_ret.shape[1]
    tm = ROW_TILE
    return pl.pallas_call(
        _merge_kernel,
        out_shape=jax.ShapeDtypeStruct((m, d), BF16),
        grid=(m // tm,),
        in_specs=[pl.BlockSpec((tm, kdim), lambda i: (i, 0)),
                  pl.BlockSpec((kdim, d), lambda i: (0, 0), pipeline_mode=pl.Buffered(1)),
                  pl.BlockSpec((tm, d), lambda i: (i, col_mb)),
                  pl.BlockSpec((tm, d), lambda i: (i, 0))],
        out_specs=pl.BlockSpec((tm, d), lambda i: (i, 0)),
        compiler_params=_params(("parallel",)),
        name="merge",
    )(act_b, w_ret, p, m_a)


def _out_kernel(mg_ref, w_ref, g_ref, *refs, nt, pad_rows, from_x, last):
    i = pl.program_id(0)
    t = i % nt
    n_in = 2 if from_x else 1
    resid = _front_or_x(t, *refs[:n_in]) if from_x else refs[0][...]
    out_refs = refs[n_in:]
    tm = resid.shape[0]
    hn = resid + jnp.dot(mg_ref[...], w_ref[...], preferred_element_type=F32)
    row = t * tm + lax.broadcasted_iota(jnp.int32, (tm, 1), 0)
    hn = jnp.where(row >= pad_rows, hn, 0.0)
    y = hn * lax.rsqrt(jnp.mean(hn * hn, axis=-1, keepdims=True) + EPS)
    if not last:
        out_refs[0][...] = hn
    out_refs[-1][...] = (y * g_ref[...]).astype(out_refs[-1].dtype)


def _out_proj(merged, w_out, g_next, resid, nb, nt, pad_rows, from_x, last, out_dtype):
    m, d = merged.shape
    tm = ROW_TILE
    row = pl.BlockSpec((tm, d), lambda i: (i, 0))
    resid_specs = [pl.BlockSpec((tm, d), lambda i: (0, 0)), _x_tile_spec(tm, d, nt)] if from_x else [row]
    if last:
        out_shape = jax.ShapeDtypeStruct((nb, (nt - 1) * tm, d), out_dtype)
        out_specs = _x_tile_spec(tm, d, nt)
    else:
        out_shape = (jax.ShapeDtypeStruct((m, d), F32), jax.ShapeDtypeStruct((m, d), BF16))
        out_specs = (row, row)
    return pl.pallas_call(
        functools.partial(_out_kernel, nt=nt, pad_rows=pad_rows, from_x=from_x, last=last),
        out_shape=out_shape,
        grid=(m // tm,),
        in_specs=[row,
                  pl.BlockSpec((d, d), lambda i: (0, 0), pipeline_mode=pl.Buffered(1)),
                  pl.BlockSpec((1, d), lambda i: (0, 0))] + resid_specs,
        out_specs=out_specs,
        compiler_params=_params(("arbitrary",)),
        name="out_proj",
    )(merged, w_out, g_next.reshape(1, d), *resid)


def kernel(x, meta_tokens, norm_g, w_in, conv_w, conv_b, conv_ln_w, conv_ln_b, conv_out,
           ret_gn_w, ret_gn_b, ret_out, w_out, final_norm_g):
    nb, seq, d = x.shape
    depth = w_in.shape[0]
    n_meta = meta_tokens.shape[0]
    heads = RET_HEADS
    conv_width = conv_out.shape[1]
    ret_width = ret_out.shape[1]
    dk = d // heads
    dv = ret_width // heads
    qk_width = heads * dk
    assert seq % ROW_TILE == 0 and n_meta <= FRONT_ROWS and conv_width == d

    cols = {}
    off = 0
    for name, width in (("glu_a", conv_width), ("glu_b", conv_width), ("conv_gate", conv_width),
                        ("q", qk_width), ("k", qk_width), ("v", ret_width), ("ret_gate", ret_width),
                        ("merge_a", d), ("merge_b", d)):
        cols[name] = off
        off += width
    assert off == w_in.shape[2]
    wide = {n: cols[n] // d for n in ("glu_a", "glu_b", "conv_gate", "merge_a", "merge_b")}

    pad_rows = FRONT_ROWS - n_meta
    lp = FRONT_ROWS + seq
    nt = lp // ROW_TILE
    m = nb * lp

    front = jnp.concatenate([jnp.zeros((pad_rows, d), x.dtype), meta_tokens.astype(x.dtype)], axis=0)

    pos = (jnp.arange(lp, dtype=jnp.int32) - pad_rows).astype(F32)
    inv_freq = 1.0 / (ROPE_BASE ** jnp.linspace(0.0, 1.0, dk // 2, dtype=F32))
    ang = pos[:, None] * inv_freq[None]
    cos, sin = jnp.cos(ang), jnp.sin(ang)
    cos2 = jnp.stack([cos, cos], axis=-1).reshape(lp, dk)
    sin2 = jnp.stack([-sin, sin], axis=-1).reshape(lp, dk)
    log_g = jnp.log(1.0 - jnp.exp2(-5.0 - jnp.arange(heads, dtype=F32)))

    tm_proj = 1280 if m % 1280 == 0 else ROW_TILE
    tr = 1280 if lp % 1280 == 0 else ROW_TILE

    hn = _first_rmsnorm(front, x, norm_g[0], nt)
    h = None
    for l in range(depth):
        p = _in_proj(hn, w_in, l, tm_proj, 1024)
        m_a = _conformer(p, conv_w[l], conv_b[l], conv_ln_w[l], conv_ln_b[l],
                         conv_out[l].astype(BF16), nb, nt, wide)
        act_b = _retention(p, cos2, sin2, log_g, ret_gn_w[l], ret_gn_b[l], nb, lp // tr, tr, cols, dk, dv)
        merged = _merge(act_b, ret_out[l].astype(BF16), p, m_a, wide["merge_b"])
        last = l == depth - 1
        g_next = final_norm_g if last else norm_g[l + 1]
        resid = (front, x) if l == 0 else (h,)
        res = _out_proj(merged, w_out[l].astype(BF16), g_next, resid, nb, nt, pad_rows,
                        from_x=(l == 0), last=last, out_dtype=x.dtype)
        if last:
            return res
        h, hn = res
```

```python
import functools

import jax
import jax.numpy as jnp
from jax import lax
from jax.experimental import pallas as pl
from jax.experimental.pallas import tpu as pltpu

F32 = jnp.float32
BF16 = jnp.bfloat16

LANES = 128
ROW_TILE = 256
FRONT_ROWS = ROW_TILE
PROJ_COLS = 512
CONV_HIST = 32
CONV_ROW_BLOCK = 64
RET_HEADS = 8
ROPE_BASE = 10000.0
EPS = 1e-6
VMEM_LIMIT = 56 * 1024 * 1024


def _sigmoid(v):
    return 0.5 * jnp.tanh(0.5 * v) + 0.5


def _silu(v):
    return v * _sigmoid(v)


def _params(sem, vmem=VMEM_LIMIT):
    return pltpu.CompilerParams(dimension_semantics=sem, vmem_limit_bytes=vmem)


def _front_or_x(t, front_ref, x_ref):
    return jnp.where(t == 0, front_ref[...], x_ref[...])


def _x_tile_spec(tm, d, nt):
    return pl.BlockSpec((None, tm, d), lambda i: (i // nt, jnp.maximum(i % nt - 1, 0), 0))


def _rmsnorm_kernel(front_ref, x_ref, g_ref, o_ref, *, nt):
    x = _front_or_x(pl.program_id(0) % nt, front_ref, x_ref)
    y = x * lax.rsqrt(jnp.mean(x * x, axis=-1, keepdims=True) + EPS)
    o_ref[...] = (y * g_ref[...]).astype(o_ref.dtype)


def _first_rmsnorm(front, x, g, nt):
    nb, _, d = x.shape
    tm = ROW_TILE
    return pl.pallas_call(
        functools.partial(_rmsnorm_kernel, nt=nt),
        out_shape=jax.ShapeDtypeStruct((nb * nt * tm, d), BF16),
        grid=(nb * nt,),
        in_specs=[pl.BlockSpec((tm, d), lambda i: (0, 0)),
                  _x_tile_spec(tm, d, nt),
                  pl.BlockSpec((1, d), lambda i: (0, 0))],
        out_specs=pl.BlockSpec((tm, d), lambda i: (i, 0)),
        compiler_params=_params(("arbitrary",)),
        name="rmsnorm",
    )(front, x, g.reshape(1, d))


def _in_proj_kernel(x_ref, wa_ref, wb_ref, cos_ref, sin_ref, u_ref, p_ref, *, bounds, dk, scale):
    j = pl.program_id(1)
    wc = wa_ref.shape[1]

    def dot_chunk(w_ref, c0):
        return jnp.dot(x_ref[...], w_ref[:, c0:c0 + dk].astype(BF16), preferred_element_type=F32)

    def store_p(fn):
        for half, w_ref in enumerate((wa_ref, wb_ref)):
            for c0 in range(0, wc, dk):
                p_ref[:, half * wc + c0:half * wc + c0 + dk] = fn(dot_chunk(w_ref, c0)).astype(p_ref.dtype)

    def rotate(r, mult):
        even = lax.broadcasted_iota(jnp.int32, (r.shape[0], LANES), 1) % 2 == 0
        parts = []
        for l0 in range(0, dk, LANES):
            xl = r[:, l0:l0 + LANES]
            nxt = pltpu.roll(xl, LANES - 1, 1)
            prv = pltpu.roll(xl, 1, 1)
            y = xl * cos_ref[:, l0:l0 + LANES] + jnp.where(even, nxt, prv) * sin_ref[:, l0:l0 + LANES]
            parts.append(y if mult is None else y * mult)
        return jnp.concatenate(parts, axis=1)

    phase = lambda name: jnp.logical_and(j >= bounds[name][0], j < bounds[name][1])

    @pl.when(phase("glu"))
    def _():
        for c0 in range(0, wc, dk):
            a = dot_chunk(wa_ref, c0)
            gl = dot_chunk(wb_ref, c0)
            u_ref[:, c0:c0 + dk] = (a * _sigmoid(gl)).astype(u_ref.dtype)

    @pl.when(jnp.logical_or(phase("conv_gate"), phase("ret_gate")))
    def _():
        store_p(_silu)

    @pl.when(phase("q"))
    def _():
        store_p(lambda r: rotate(r, scale))

    @pl.when(phase("k"))
    def _():
        store_p(lambda r: rotate(r, None))

    @pl.when(phase("v"))
    def _():
        store_p(lambda r: r)

    @pl.when(jnp.logical_or(phase("merge_a"), phase("merge_b")))
    def _():
        store_p(_sigmoid)


def _in_proj(hn, w_all, layer, cos2, sin2, tm, widths, dk):
    m, k = hn.shape
    wc = PROJ_COLS
    ntm = cos2.shape[0] // tm
    n_glu = widths["glu_a"] // wc
    assert widths["glu_a"] == widths["glu_b"]
    bounds = {"glu": (0, n_glu)}
    step = n_glu
    for name in ("conv_gate", "q", "k", "v", "ret_gate", "merge_a", "merge_b"):
        n = widths[name] // (2 * wc)
        bounds[name] = (step, step + n)
        step += n
    n_p = step - n_glu
    blk0 = 2 * n_glu
    wa_map = lambda i, j: (layer, 0, jnp.where(j < n_glu, j, blk0 + 2 * (j - n_glu)))
    wb_map = lambda i, j: (layer, 0, jnp.where(j < n_glu, n_glu + j, blk0 + 2 * (j - n_glu) + 1))
    tab = pl.BlockSpec((tm, dk), lambda i, j: (i % ntm, 0))
    return pl.pallas_call(
        functools.partial(_in_proj_kernel, bounds=bounds, dk=dk, scale=dk ** -0.5),
        out_shape=(jax.ShapeDtypeStruct((m, n_glu * wc), BF16),
                   jax.ShapeDtypeStruct((m, n_p * 2 * wc), BF16)),
        grid=(m // tm, step),
        in_specs=[pl.BlockSpec((tm, k), lambda i, j: (i, 0)),
                  pl.BlockSpec((None, k, wc), wa_map),
                  pl.BlockSpec((None, k, wc), wb_map),
                  tab, tab],
        out_specs=(pl.BlockSpec((tm, wc), lambda i, j: (i, jnp.minimum(j, n_glu - 1))),
                   pl.BlockSpec((tm, 2 * wc), lambda i, j: (i, jnp.maximum(j - n_glu, 0)))),
        compiler_params=_params(("arbitrary", "arbitrary")),
        name="in_proj",
    )(hn, w_all, w_all, cos2, sin2)


def _conformer_kernel(u_ref, sgate_ref, sma_ref, cw_ref, cb_ref, lnw_ref, lnb_ref,
                      wproj_ref, o_ref, uext_ref, conv_ref, *, kw):
    t = pl.program_id(1)
    tt = u_ref.shape[0]
    d = u_ref.shape[1]
    ncb = d // LANES
    first = CONV_HIST - (kw - 1)

    @pl.when(t == 0)
    def _():
        uext_ref[:, 0:CONV_HIST, :] = jnp.zeros((ncb, CONV_HIST, LANES), F32)

    @pl.when(t > 0)
    def _():
        uext_ref[:, 0:CONV_HIST, :] = uext_ref[:, tt:tt + CONV_HIST, :]

    for c in range(ncb):
        uext_ref[c, CONV_HIST:CONV_HIST + tt, :] = u_ref[:, c * LANES:(c + 1) * LANES].astype(F32)

    def conv_block(c, carry):
        for rb in range(tt // CONV_ROW_BLOCK):
            r0 = rb * CONV_ROW_BLOCK
            acc = jnp.broadcast_to(cb_ref[c], (CONV_ROW_BLOCK, LANES))
            for k in range(kw):
                acc = acc + cw_ref[c, k:k + 1, :] * uext_ref[c, r0 + first + k:r0 + first + k + CONV_ROW_BLOCK, :]
            conv_ref[c, r0:r0 + CONV_ROW_BLOCK, :] = acc
        return carry

    lax.fori_loop(0, ncb, conv_block, 0)

    cv = conv_ref[...]
    mu = jnp.sum(jnp.sum(cv, axis=0), axis=-1, keepdims=True) / d
    dev = cv - mu[None]
    var = jnp.sum(jnp.sum(dev * dev, axis=0), axis=-1, keepdims=True) / d
    y = dev * lax.rsqrt(var + EPS)[None] * lnw_ref[...] + lnb_ref[...]
    act = _silu(y)
    lhs = jnp.concatenate(
        [(act[c] * sgate_ref[:, c * LANES:(c + 1) * LANES].astype(F32)).astype(BF16) for c in range(ncb)],
        axis=1)
    ya = jnp.dot(lhs, wproj_ref[...], preferred_element_type=F32)
    o_ref[...] = sma_ref[...].astype(F32) * ya


def _conformer(u, p, conv_w, conv_b, ln_w, ln_b, w_proj, nb, nt, pcol):
    m, d = u.shape
    kw = conv_w.shape[0]
    ncb = d // LANES
    tt = ROW_TILE
    cw = conv_w.reshape(kw, ncb, LANES).transpose(1, 0, 2)
    blk = lambda col: pl.BlockSpec((tt, d), lambda b, t, col=col: (b * nt + t, col))
    vec3 = pl.BlockSpec((ncb, 1, LANES), lambda b, t: (0, 0, 0))
    return pl.pallas_call(
        functools.partial(_conformer_kernel, kw=kw),
        out_shape=jax.ShapeDtypeStruct((m, d), F32),
        grid=(nb, nt),
        in_specs=[blk(0), blk(pcol["conv_gate"] // d), blk(pcol["merge_a"] // d),
                  pl.BlockSpec((ncb, kw, LANES), lambda b, t: (0, 0, 0)),
                  vec3, vec3, vec3,
                  pl.BlockSpec((d, d), lambda b, t: (0, 0), pipeline_mode=pl.Buffered(1))],
        out_specs=pl.BlockSpec((tt, d), lambda b, t: (b * nt + t, 0)),
        scratch_shapes=[pltpu.VMEM((ncb, CONV_HIST + tt, LANES), F32),
                        pltpu.VMEM((ncb, tt, LANES), F32)],
        compiler_params=_params(("parallel", "arbitrary")),
        name="conformer",
    )(u, p, p, cw, conv_b.reshape(ncb, 1, LANES), ln_w.reshape(ncb, 1, LANES),
      ln_b.reshape(ncb, 1, LANES), w_proj)


def _retention_kernel(logg_ref, q_ref, k_ref, v_ref, sg_ref, gnw_ref, gnb_ref,
                      o_ref, state_ref, decay_ref, xi_ref, zeta_ref, *, chunk):
    t = pl.program_id(0)
    b = pl.program_id(1)
    h = pl.program_id(2)
    nsub = q_ref.shape[0] // chunk
    lg = logg_ref[h]
    slot = b * pl.num_programs(2) + h

    @pl.when(jnp.logical_and(t == 0, b == 0))
    def _():
        i = lax.broadcasted_iota(jnp.int32, (chunk, chunk), 0)
        j = lax.broadcasted_iota(jnp.int32, (chunk, chunk), 1)
        dist = (i - j).astype(F32)
        decay_ref[h] = jnp.where(dist >= 0, jnp.exp(lg * jnp.maximum(dist, 0.0)), 0.0)
        idx = lax.broadcasted_iota(jnp.int32, (chunk, 1), 0).astype(F32)
        xi_ref[h] = jnp.exp(lg * (idx + 1.0))
        zeta_ref[h] = jnp.exp(lg * (chunk - 1.0 - idx))

    @pl.when(t == 0)
    def _():
        state_ref[slot] = jnp.zeros(state_ref.shape[1:], F32)

    g_chunk = jnp.exp(jnp.full((1, 1), lg * chunk, F32))

    def step(c, carry):
        rows = pl.ds(pl.multiple_of(c * chunk, chunk), chunk)
        qb = q_ref[rows, :]
        kb = k_ref[rows, :]
        v = v_ref[rows, :]
        state = state_ref[slot]
        s = lax.dot_general(qb, kb, (((1,), (1,)), ((), ())), preferred_element_type=F32) * decay_ref[h]
        inner = jnp.dot(s.astype(BF16), v, preferred_element_type=F32)
        cross = jnp.dot(qb, state.astype(BF16), preferred_element_type=F32) * xi_ref[h]
        kz = (kb.astype(F32) * zeta_ref[h]).astype(BF16)
        state_ref[slot] = state * g_chunk + lax.dot_general(
            kz, v, (((0,), (0,)), ((), ())), preferred_element_type=F32)
        o = inner + cross
        mu = jnp.mean(o, axis=-1, keepdims=True)
        dev = o - mu
        var = jnp.mean(dev * dev, axis=-1, keepdims=True)
        y = dev * lax.rsqrt(var + EPS) * gnw_ref[...] + gnb_ref[...]
        o_ref[rows, :] = (y * sg_ref[rows, :].astype(F32)).astype(o_ref.dtype)
        return carry

    lax.fori_loop(0, nsub, step, 0)


def _retention(p, log_g, gn_w, gn_b, nb, nt, tr, pcol, dk, dv):
    m = p.shape[0]
    heads = RET_HEADS
    chunk = ROW_TILE
    qblk = lambda col: pl.BlockSpec((tr, dk), lambda t, b, h, col=col: (b * nt + t, col + h))
    vblk = lambda col: pl.BlockSpec((tr, dv), lambda t, b, h, col=col: (b * nt + t, col + h))
    vec = pl.BlockSpec((1, dv), lambda t, b, h: (0, h))
    return pl.pallas_call(
        functools.partial(_retention_kernel, chunk=chunk),
        out_shape=jax.ShapeDtypeStruct((m, heads * dv), BF16),
        grid=(nt, nb, heads),
        in_specs=[pl.BlockSpec(memory_space=pltpu.SMEM),
                  qblk(pcol["q"] // dk), qblk(pcol["k"] // dk),
                  vblk(pcol["v"] // dv), vblk(pcol["ret_gate"] // dv),
                  vec, vec],
        out_specs=pl.BlockSpec((tr, dv), lambda t, b, h: (b * nt + t, h)),
        scratch_shapes=[pltpu.VMEM((nb * heads, dk, dv), F32),
                        pltpu.VMEM((heads, chunk, chunk), F32),
                        pltpu.VMEM((heads, chunk, 1), F32),
                        pltpu.VMEM((heads, chunk, 1), F32)],
        compiler_params=_params(("arbitrary", "arbitrary", "arbitrary")),
        name="retention",
    )(log_g, p, p, p, p, gn_w.reshape(1, -1), gn_b.reshape(1, -1))


def _tail_kernel(act_ref, wret_ref, smb_ref, ma_ref, wout_ref, g_ref, *refs, nt, pad_rows, from_x, last):
    i = pl.program_id(0)
    t = i % nt
    n_in = 2 if from_x else 1
    resid = _front_or_x(t, *refs[:n_in]) if from_x else refs[0][...]
    out_refs = refs[n_in:]
    tm = resid.shape[0]
    yb = jnp.dot(act_ref[...], wret_ref[...], preferred_element_type=F32)
    merged = (ma_ref[...] + smb_ref[...].astype(F32) * yb).astype(BF16)
    hn = resid + jnp.dot(merged, wout_ref[...], preferred_element_type=F32)
    row = t * tm + lax.broadcasted_iota(jnp.int32, (tm, 1), 0)
    hn = jnp.where(row >= pad_rows, hn, 0.0)
    y = hn * lax.rsqrt(jnp.mean(hn * hn, axis=-1, keepdims=True) + EPS)
    if not last:
        out_refs[0][...] = hn
    out_refs[-1][...] = (y * g_ref[...]).astype(out_refs[-1].dtype)


def _tail(act_b, w_ret, p, col_mb, m_a, w_out, g_next, resid, nb, nt, pad_rows, from_x, last, out_dtype):
    m, d = m_a.shape
    kdim = act_b.shape[1]
    tm = ROW_TILE
    row = pl.BlockSpec((tm, d), lambda i: (i, 0))
    whole = lambda r, c: pl.BlockSpec((r, c), lambda i: (0, 0), pipeline_mode=pl.Buffered(1))
    resid_specs = [pl.BlockSpec((tm, d), lambda i: (0, 0)), _x_tile_spec(tm, d, nt)] if from_x else [row]
    if last:
        out_shape = jax.ShapeDtypeStruct((nb, (nt - 1) * tm, d), out_dtype)
        out_specs = _x_tile_spec(tm, d, nt)
    else:
        out_shape = (jax.ShapeDtypeStruct((m, d), F32), jax.ShapeDtypeStruct((m, d), BF16))
        out_specs = (row, row)
    return pl.pallas_call(
        functools.partial(_tail_kernel, nt=nt, pad_rows=pad_rows, from_x=from_x, last=last),
        out_shape=out_shape,
        grid=(m // tm,),
        in_specs=[pl.BlockSpec((tm, kdim), lambda i: (i, 0)),
                  whole(kdim, d),
                  pl.BlockSpec((tm, d), lambda i: (i, col_mb)),
                  row,
                  whole(d, d),
                  pl.BlockSpec((1, d), lambda i: (0, 0))] + resid_specs,
        out_specs=out_specs,
        compiler_params=_params(("arbitrary",)),
        name="tail",
    )(act_b, w_ret, p, m_a, w_out, g_next.reshape(1, d), *resid)


def kernel(x, meta_tokens, norm_g, w_in, conv_w, conv_b, conv_ln_w, conv_ln_b, conv_out,
           ret_gn_w, ret_gn_b, ret_out, w_out, final_norm_g):
    nb, seq, d = x.shape
    depth = w_in.shape[0]
    n_meta = meta_tokens.shape[0]
    heads = RET_HEADS
    conv_width = conv_out.shape[1]
    ret_width = ret_out.shape[1]
    dk = d // heads
    dv = ret_width // heads
    qk_width = heads * dk
    assert seq % ROW_TILE == 0 and n_meta <= FRONT_ROWS and conv_width == d

    widths = {"glu_a": conv_width, "glu_b": conv_width, "conv_gate": conv_width, "q": qk_width, "k": qk_width,
              "v": ret_width, "ret_gate": ret_width, "merge_a": d, "merge_b": d}
    assert sum(widths.values()) == w_in.shape[2]
    pcol = {}
    off = 0
    for name in ("conv_gate", "q", "k", "v", "ret_gate", "merge_a", "merge_b"):
        pcol[name] = off
        off += widths[name]

    pad_rows = FRONT_ROWS - n_meta
    lp = FRONT_ROWS + seq
    nt = lp // ROW_TILE
    m = nb * lp

    front = jnp.concatenate([jnp.zeros((pad_rows, d), x.dtype), meta_tokens.astype(x.dtype)], axis=0)

    pos = (jnp.arange(lp, dtype=jnp.int32) - pad_rows).astype(F32)
    inv_freq = 1.0 / (ROPE_BASE ** jnp.linspace(0.0, 1.0, dk // 2, dtype=F32))
    ang = pos[:, None] * inv_freq[None]
    cos, sin = jnp.cos(ang), jnp.sin(ang)
    cos2 = jnp.stack([cos, cos], axis=-1).reshape(lp, dk)
    sin2 = jnp.stack([-sin, sin], axis=-1).reshape(lp, dk)
    log_g = jnp.log(1.0 - jnp.exp2(-5.0 - jnp.arange(heads, dtype=F32)))

    tm_proj = 1280 if lp % 1280 == 0 else ROW_TILE
    tr = 1280 if lp % 1280 == 0 else ROW_TILE

    hn = _first_rmsnorm(front, x, norm_g[0], nt)
    h = None
    for l in range(depth):
        u, p = _in_proj(hn, w_in, l, cos2, sin2, tm_proj, widths, dk)
        m_a = _conformer(u, p, conv_w[l], conv_b[l], conv_ln_w[l], conv_ln_b[l],
                         conv_out[l].astype(BF16), nb, nt, pcol)
        act_b = _retention(p, log_g, ret_gn_w[l], ret_gn_b[l], nb, lp // tr, tr, pcol, dk, dv)
        last = l == depth - 1
        g_next = final_norm_g if last else norm_g[l + 1]
        resid = (front, x) if l == 0 else (h,)
        res = _tail(act_b, ret_out[l].astype(BF16), p, pcol["merge_b"] // d, m_a, w_out[l].astype(BF16),
                    g_next, resid, nb, nt, pad_rows, from_x=(l == 0), last=last, out_dtype=x.dtype)
        if last:
            return res
        h, hn = res
```

```python
import functools

import jax
import jax.numpy as jnp
from jax import lax
from jax.experimental import pallas as pl
from jax.experimental.pallas import tpu as pltpu

F32 = jnp.float32
BF16 = jnp.bfloat16

LANES = 128
ROW_TILE = 256
FRONT_ROWS = ROW_TILE
PROJ_COLS = 512
PROJ_ROW_SPLIT = 4
CONV_HIST = 32
CONV_ROW_BLOCK = 64
RET_HEADS = 8
ROPE_BASE = 10000.0
EPS = 1e-6
VMEM_LIMIT = 56 * 1024 * 1024


def _sigmoid(v):
    return 0.5 * jnp.tanh(0.5 * v) + 0.5


def _silu(v):
    return v * _sigmoid(v)


def _params(sem, vmem=VMEM_LIMIT):
    return pltpu.CompilerParams(dimension_semantics=sem, vmem_limit_bytes=vmem)


def _front_or_x(t, front_ref, x_ref):
    return jnp.where(t == 0, front_ref[...], x_ref[...])


def _x_tile_spec(tm, d, nt):
    return pl.BlockSpec((None, tm, d), lambda i: (i // nt, jnp.maximum(i % nt - 1, 0), 0))


def _rmsnorm_kernel(front_ref, x_ref, g_ref, o_ref, *, nt):
    x = _front_or_x(pl.program_id(0) % nt, front_ref, x_ref)
    y = x * lax.rsqrt(jnp.mean(x * x, axis=-1, keepdims=True) + EPS)
    o_ref[...] = (y * g_ref[...]).astype(o_ref.dtype)


def _first_rmsnorm(front, x, g, nt):
    nb, _, d = x.shape
    tm = ROW_TILE
    return pl.pallas_call(
        functools.partial(_rmsnorm_kernel, nt=nt),
        out_shape=jax.ShapeDtypeStruct((nb * nt * tm, d), BF16),
        grid=(nb * nt,),
        in_specs=[pl.BlockSpec((tm, d), lambda i: (0, 0)),
                  _x_tile_spec(tm, d, nt),
                  pl.BlockSpec((1, d), lambda i: (0, 0))],
        out_specs=pl.BlockSpec((tm, d), lambda i: (i, 0)),
        compiler_params=_params(("arbitrary",)),
        name="rmsnorm",
    )(front, x, g.reshape(1, d))


def _in_proj_kernel(x_ref, wa_ref, wb_ref, cos_ref, sin_ref, u_ref, p_ref, *, bounds, dk, scale):
    j = pl.program_id(1)
    wc = wa_ref.shape[1]

    tm = x_ref.shape[0]
    rh = tm // PROJ_ROW_SPLIT

    def dot_chunk(w_ref, r0, c0):
        return jnp.dot(x_ref[r0:r0 + rh, :], w_ref[:, c0:c0 + dk].astype(BF16), preferred_element_type=F32)

    def store_p(fn):
        for half, w_ref in enumerate((wa_ref, wb_ref)):
            for c0 in range(0, wc, dk):
                for r0 in range(0, tm, rh):
                    p_ref[r0:r0 + rh, half * wc + c0:half * wc + c0 + dk] = fn(
                        dot_chunk(w_ref, r0, c0), r0).astype(p_ref.dtype)

    def rotate(r, r0, mult):
        even = lax.broadcasted_iota(jnp.int32, (r.shape[0], LANES), 1) % 2 == 0
        parts = []
        for l0 in range(0, dk, LANES):
            xl = r[:, l0:l0 + LANES]
            nxt = pltpu.roll(xl, LANES - 1, 1)
            prv = pltpu.roll(xl, 1, 1)
            y = (xl * cos_ref[r0:r0 + rh, l0:l0 + LANES]
                 + jnp.where(even, nxt, prv) * sin_ref[r0:r0 + rh, l0:l0 + LANES])
            parts.append(y if mult is None else y * mult)
        return jnp.concatenate(parts, axis=1)

    phase = lambda name: jnp.logical_and(j >= bounds[name][0], j < bounds[name][1])

    @pl.when(phase("glu"))
    def _():
        for c0 in range(0, wc, dk):
            for r0 in range(0, tm, rh):
                a = dot_chunk(wa_ref, r0, c0)
                gl = dot_chunk(wb_ref, r0, c0)
                u_ref[r0:r0 + rh, c0:c0 + dk] = (a * _sigmoid(gl)).astype(u_ref.dtype)

    @pl.when(jnp.logical_or(phase("conv_gate"), phase("ret_gate")))
    def _():
        store_p(lambda r, r0: _silu(r))

    @pl.when(phase("q"))
    def _():
        store_p(lambda r, r0: rotate(r, r0, scale))

    @pl.when(phase("k"))
    def _():
        store_p(lambda r, r0: rotate(r, r0, None))

    @pl.when(phase("v"))
    def _():
        store_p(lambda r, r0: r)

    @pl.when(jnp.logical_or(phase("merge_a"), phase("merge_b")))
    def _():
        store_p(lambda r, r0: _sigmoid(r))


def _in_proj(hn, w_all, layer, cos2, sin2, tm, widths, dk):
    m, k = hn.shape
    wc = PROJ_COLS
    ntm = cos2.shape[0] // tm
    n_glu = widths["glu_a"] // wc
    assert widths["glu_a"] == widths["glu_b"]
    bounds = {"glu": (0, n_glu)}
    step = n_glu
    for name in ("conv_gate", "q", "k", "v", "ret_gate", "merge_a", "merge_b"):
        n = widths[name] // (2 * wc)
        bounds[name] = (step, step + n)
        step += n
    n_p = step - n_glu
    blk0 = 2 * n_glu
    wa_map = lambda i, j: (layer, 0, jnp.where(j < n_glu, j, blk0 + 2 * (j - n_glu)))
    wb_map = lambda i, j: (layer, 0, jnp.where(j < n_glu, n_glu + j, blk0 + 2 * (j - n_glu) + 1))
    tab = pl.BlockSpec((tm, dk), lambda i, j: (i % ntm, 0))
    return pl.pallas_call(
        functools.partial(_in_proj_kernel, bounds=bounds, dk=dk, scale=dk ** -0.5),
        out_shape=(jax.ShapeDtypeStruct((m, n_glu * wc), BF16),
                   jax.ShapeDtypeStruct((m, n_p * 2 * wc), BF16)),
        grid=(m // tm, step),
        in_specs=[pl.BlockSpec((tm, k), lambda i, j: (i, 0)),
                  pl.BlockSpec((None, k, wc), wa_map),
                  pl.BlockSpec((None, k, wc), wb_map),
                  tab, tab],
        out_specs=(pl.BlockSpec((tm, wc), lambda i, j: (i, jnp.minimum(j, n_glu - 1))),
                   pl.BlockSpec((tm, 2 * wc), lambda i, j: (i, jnp.maximum(j - n_glu, 0)))),
        compiler_params=_params(("arbitrary", "arbitrary")),
        name="in_proj",
    )(hn, w_all, w_all, cos2, sin2)


def _conformer_kernel(u_ref, sgate_ref, sma_ref, cw_ref, cb_ref, lnw_ref, lnb_ref,
                      wproj_ref, o_ref, uext_ref, conv_ref, *, kw):
    t = pl.program_id(1)
    tt = u_ref.shape[0]
    d = u_ref.shape[1]
    ncb = d // LANES
    first = CONV_HIST - (kw - 1)

    @pl.when(t == 0)
    def _():
        uext_ref[:, 0:CONV_HIST, :] = jnp.zeros((ncb, CONV_HIST, LANES), F32)

    @pl.when(t > 0)
    def _():
        uext_ref[:, 0:CONV_HIST, :] = uext_ref[:, tt:tt + CONV_HIST, :]

    for c in range(ncb):
        uext_ref[c, CONV_HIST:CONV_HIST + tt, :] = u_ref[:, c * LANES:(c + 1) * LANES].astype(F32)

    def conv_block(c, carry):
        for rb in range(tt // CONV_ROW_BLOCK):
            r0 = rb * CONV_ROW_BLOCK
            acc = jnp.broadcast_to(cb_ref[c], (CONV_ROW_BLOCK, LANES))
            for k in range(kw):
                acc = acc + cw_ref[c, k:k + 1, :] * uext_ref[c, r0 + first + k:r0 + first + k + CONV_ROW_BLOCK, :]
            conv_ref[c, r0:r0 + CONV_ROW_BLOCK, :] = acc
        return carry

    lax.fori_loop(0, ncb, conv_block, 0)

    cv = conv_ref[...]
    mu = jnp.sum(jnp.sum(cv, axis=0), axis=-1, keepdims=True) / d
    dev = cv - mu[None]
    var = jnp.sum(jnp.sum(dev * dev, axis=0), axis=-1, keepdims=True) / d
    y = dev * lax.rsqrt(var + EPS)[None] * lnw_ref[...] + lnb_ref[...]
    act = _silu(y)
    lhs = jnp.concatenate(
        [(act[c] * sgate_ref[:, c * LANES:(c + 1) * LANES].astype(F32)).astype(BF16) for c in range(ncb)],
        axis=1)
    ya = jnp.dot(lhs, wproj_ref[...], preferred_element_type=F32)
    o_ref[...] = sma_ref[...].astype(F32) * ya


def _conformer(u, p, conv_w, conv_b, ln_w, ln_b, w_proj, nb, nt, pcol):
    m, d = u.shape
    kw = conv_w.shape[0]
    ncb = d // LANES
    tt = ROW_TILE
    cw = conv_w.reshape(kw, ncb, LANES).transpose(1, 0, 2)
    blk = lambda col: pl.BlockSpec((tt, d), lambda b, t, col=col: (b * nt + t, col))
    vec3 = pl.BlockSpec((ncb, 1, LANES), lambda b, t: (0, 0, 0))
    return pl.pallas_call(
        functools.partial(_conformer_kernel, kw=kw),
        out_shape=jax.ShapeDtypeStruct((m, d), F32),
        grid=(nb, nt),
        in_specs=[blk(0), blk(pcol["conv_gate"] // d), blk(pcol["merge_a"] // d),
                  pl.BlockSpec((ncb, kw, LANES), lambda b, t: (0, 0, 0)),
                  vec3, vec3, vec3,
                  pl.BlockSpec((d, d), lambda b, t: (0, 0), pipeline_mode=pl.Buffered(1))],
        out_specs=pl.BlockSpec((tt, d), lambda b, t: (b * nt + t, 0)),
        scratch_shapes=[pltpu.VMEM((ncb, CONV_HIST + tt, LANES), F32),
                        pltpu.VMEM((ncb, tt, LANES), F32)],
        compiler_params=_params(("parallel", "arbitrary")),
        name="conformer",
    )(u, p, p, cw, conv_b.reshape(ncb, 1, LANES), ln_w.reshape(ncb, 1, LANES),
      ln_b.reshape(ncb, 1, LANES), w_proj)


def _retention_kernel(logg_ref, q_ref, k_ref, v_ref, sg_ref, gnw_ref, gnb_ref,
                      o_ref, state_ref, decay_ref, xi_ref, zeta_ref, *, chunk):
    t = pl.program_id(0)
    b = pl.program_id(1)
    h = pl.program_id(2)
    nsub = q_ref.shape[0] // chunk
    lg = logg_ref[h]
    slot = b * pl.num_programs(2) + h

    @pl.when(jnp.logical_and(t == 0, b == 0))
    def _():
        i = lax.broadcasted_iota(jnp.int32, (chunk, chunk), 0)
        j = lax.broadcasted_iota(jnp.int32, (chunk, chunk), 1)
        dist = (i - j).astype(F32)
        decay_ref[h] = jnp.where(dist >= 0, jnp.exp(lg * jnp.maximum(dist, 0.0)), 0.0)
        idx = lax.broadcasted_iota(jnp.int32, (chunk, 1), 0).astype(F32)
        xi_ref[h] = jnp.exp(lg * (idx + 1.0))
        zeta_ref[h] = jnp.exp(lg * (chunk - 1.0 - idx))

    @pl.when(t == 0)
    def _():
        state_ref[slot] = jnp.zeros(state_ref.shape[1:], F32)

    g_chunk = jnp.exp(jnp.full((1, 1), lg * chunk, F32))

    def step(c, carry):
        rows = pl.ds(pl.multiple_of(c * chunk, chunk), chunk)
        qb = q_ref[rows, :]
        kb = k_ref[rows, :]
        v = v_ref[rows, :]
        state = state_ref[slot]
        s = lax.dot_general(qb, kb, (((1,), (1,)), ((), ())), preferred_element_type=F32) * decay_ref[h]
        inner = jnp.dot(s.astype(BF16), v, preferred_element_type=F32)
        cross = jnp.dot(qb, state.astype(BF16), preferred_element_type=F32) * xi_ref[h]
        kz = (kb.astype(F32) * zeta_ref[h]).astype(BF16)
        state_ref[slot] = state * g_chunk + lax.dot_general(
            kz, v, (((0,), (0,)), ((), ())), preferred_element_type=F32)
        o = inner + cross
        mu = jnp.mean(o, axis=-1, keepdims=True)
        dev = o - mu
        var = jnp.mean(dev * dev, axis=-1, keepdims=True)
        y = dev * lax.rsqrt(var + EPS) * gnw_ref[...] + gnb_ref[...]
        o_ref[rows, :] = (y * sg_ref[rows, :].astype(F32)).astype(o_ref.dtype)
        return carry

    lax.fori_loop(0, nsub, step, 0)


def _retention(p, log_g, gn_w, gn_b, nb, nt, tr, pcol, dk, dv):
    m = p.shape[0]
    heads = RET_HEADS
    chunk = ROW_TILE
    qblk = lambda col: pl.BlockSpec((tr, dk), lambda t, b, h, col=col: (b * nt + t, col + h))
    vblk = lambda col: pl.BlockSpec((tr, dv), lambda t, b, h, col=col: (b * nt + t, col + h))
    vec = pl.BlockSpec((1, dv), lambda t, b, h: (0, h))
    return pl.pallas_call(
        functools.partial(_retention_kernel, chunk=chunk),
        out_shape=jax.ShapeDtypeStruct((m, heads * dv), BF16),
        grid=(nt, nb, heads),
        in_specs=[pl.BlockSpec(memory_space=pltpu.SMEM),
                  qblk(pcol["q"] // dk), qblk(pcol["k"] // dk),
                  vblk(pcol["v"] // dv), vblk(pcol["ret_gate"] // dv),
                  vec, vec],
        out_specs=pl.BlockSpec((tr, dv), lambda t, b, h: (b * nt + t, h)),
        scratch_shapes=[pltpu.VMEM((nb * heads, dk, dv), F32),
                        pltpu.VMEM((heads, chunk, chunk), F32),
                        pltpu.VMEM((heads, chunk, 1), F32),
                        pltpu.VMEM((heads, chunk, 1), F32)],
        compiler_params=_params(("arbitrary", "arbitrary", "arbitrary")),
        name="retention",
    )(log_g, p, p, p, p, gn_w.reshape(1, -1), gn_b.reshape(1, -1))


def _tail_kernel(act_ref, wret_ref, smb_ref, ma_ref, wout_ref, g_ref, *refs, nt, pad_rows, from_x, last):
    i = pl.program_id(0)
    t = i % nt
    n_in = 2 if from_x else 1
    resid = _front_or_x(t, *refs[:n_in]) if from_x else refs[0][...]
    out_refs = refs[n_in:]
    tm = resid.shape[0]
    yb = jnp.dot(act_ref[...], wret_ref[...], preferred_element_type=F32)
    merged = (ma_ref[...] + smb_ref[...].astype(F32) * yb).astype(BF16)
    hn = resid + jnp.dot(merged, wout_ref[...], preferred_element_type=F32)
    row = t * tm + lax.broadcasted_iota(jnp.int32, (tm, 1), 0)
    hn = jnp.where(row >= pad_rows, hn, 0.0)
    y = hn * lax.rsqrt(jnp.mean(hn * hn, axis=-1, keepdims=True) + EPS)
    if not last:
        out_refs[0][...] = hn
    out_refs[-1][...] = (y * g_ref[...]).astype(out_refs[-1].dtype)


def _tail(act_b, w_ret, p, col_mb, m_a, w_out, g_next, resid, nb, nt, pad_rows, from_x, last, out_dtype):
    m, d = m_a.shape
    kdim = act_b.shape[1]
    tm = ROW_TILE
    row = pl.BlockSpec((tm, d), lambda i: (i, 0))
    whole = lambda r, c: pl.BlockSpec((r, c), lambda i: (0, 0), pipeline_mode=pl.Buffered(1))
    resid_specs = [pl.BlockSpec((tm, d), lambda i: (0, 0)), _x_tile_spec(tm, d, nt)] if from_x else [row]
    if last:
        out_shape = jax.ShapeDtypeStruct((nb, (nt - 1) * tm, d), out_dtype)
        out_specs = _x_tile_spec(tm, d, nt)
    else:
        out_shape = (jax.ShapeDtypeStruct((m, d), F32), jax.ShapeDtypeStruct((m, d), BF16))
        out_specs = (row, row)
    return pl.pallas_call(
        functools.partial(_tail_kernel, nt=nt, pad_rows=pad_rows, from_x=from_x, last=last),
        out_shape=out_shape,
        grid=(m // tm,),
        in_specs=[pl.BlockSpec((tm, kdim), lambda i: (i, 0)),
                  whole(kdim, d),
                  pl.BlockSpec((tm, d), lambda i: (i, col_mb)),
                  row,
                  whole(d, d),
                  pl.BlockSpec((1, d), lambda i: (0, 0))] + resid_specs,
        out_specs=out_specs,
        compiler_params=_params(("arbitrary",)),
        name="tail",
    )(act_b, w_ret, p, m_a, w_out, g_next.reshape(1, d), *resid)


def kernel(x, meta_tokens, norm_g, w_in, conv_w, conv_b, conv_ln_w, conv_ln_b, conv_out,
           ret_gn_w, ret_gn_b, ret_out, w_out, final_norm_g):
    nb, seq, d = x.shape
    depth = w_in.shape[0]
    n_meta = meta_tokens.shape[0]
    heads = RET_HEADS
    conv_width = conv_out.shape[1]
    ret_width = ret_out.shape[1]
    dk = d // heads
    dv = ret_width // heads
    qk_width = heads * dk
    assert seq % ROW_TILE == 0 and n_meta <= FRONT_ROWS and conv_width == d

    widths = {"glu_a": conv_width, "glu_b": conv_width, "conv_gate": conv_width, "q": qk_width, "k": qk_width,
              "v": ret_width, "ret_gate": ret_width, "merge_a": d, "merge_b": d}
    assert sum(widths.values()) == w_in.shape[2]
    pcol = {}
    off = 0
    for name in ("conv_gate", "q", "k", "v", "ret_gate", "merge_a", "merge_b"):
        pcol[name] = off
        off += widths[name]

    pad_rows = FRONT_ROWS - n_meta
    lp = FRONT_ROWS + seq
    nt = lp // ROW_TILE
    m = nb * lp

    front = jnp.concatenate([jnp.zeros((pad_rows, d), x.dtype), meta_tokens.astype(x.dtype)], axis=0)

    pos = (jnp.arange(lp, dtype=jnp.int32) - pad_rows).astype(F32)
    inv_freq = 1.0 / (ROPE_BASE ** jnp.linspace(0.0, 1.0, dk // 2, dtype=F32))
    ang2 = pos[:, None] * jnp.repeat(inv_freq, 2)[None]
    cos2 = jnp.cos(ang2)
    sin2 = jnp.sin(ang2) * jnp.tile(jnp.array([-1.0, 1.0], F32), dk // 2)[None]
    log_g = jnp.log(1.0 - jnp.exp2(-5.0 - jnp.arange(heads, dtype=F32)))

    tm_proj = 1280 if lp % 1280 == 0 else ROW_TILE
    tr = next(c for c in (3328, 1280, ROW_TILE) if lp % c == 0)

    hn = _first_rmsnorm(front, x, norm_g[0], nt)
    h = None
    for l in range(depth):
        u, p = _in_proj(hn, w_in, l, cos2, sin2, tm_proj, widths, dk)
        m_a = _conformer(u, p, conv_w[l], conv_b[l], conv_ln_w[l], conv_ln_b[l],
                         conv_out[l].astype(BF16), nb, nt, pcol)
        act_b = _retention(p, log_g, ret_gn_w[l], ret_gn_b[l], nb, lp // tr, tr, pcol, dk, dv)
        last = l == depth - 1
        g_next = final_norm_g if last else norm_g[l + 1]
        resid = (front, x) if l == 0 else (h,)
        res = _tail(act_b, ret_out[l].astype(BF16), p, pcol["merge_b"] // d, m_a, w_out[l].astype(BF16),
                    g_next, resid, nb, nt, pad_rows, from_x=(l == 0), last=last, out_dtype=x.dtype)
        if last:
            return res
        h, hn = res
```

```python
import functools

import jax
import jax.numpy as jnp
from jax import lax
from jax.experimental import pallas as pl
from jax.experimental.pallas import tpu as pltpu

F32 = jnp.float32
BF16 = jnp.bfloat16

LANES = 128
ROW_TILE = 256
FRONT_ROWS = ROW_TILE
PROJ_COLS = 512
PROJ_ROW_SPLIT = 4
CONV_HIST = 32
CONV_ROW_BLOCK = 64
RET_HEADS = 8
ROPE_BASE = 10000.0
EPS = 1e-6
VMEM_LIMIT = 56 * 1024 * 1024


def _sigmoid(v):
    return 0.5 * jnp.tanh(0.5 * v) + 0.5


def _silu(v):
    return v * _sigmoid(v)


def _params(sem, vmem=VMEM_LIMIT):
    return pltpu.CompilerParams(dimension_semantics=sem, vmem_limit_bytes=vmem)


def _front_or_x(t, front_ref, x_ref):
    return jnp.where(t == 0, front_ref[...], x_ref[...])


def _x_tile_spec(tm, d, nt):
    return pl.BlockSpec((None, tm, d), lambda i: (i // nt, jnp.maximum(i % nt - 1, 0), 0))


def _rmsnorm_kernel(front_ref, x_ref, g_ref, o_ref, *, nt):
    x = _front_or_x(pl.program_id(0) % nt, front_ref, x_ref)
    y = x * lax.rsqrt(jnp.mean(x * x, axis=-1, keepdims=True) + EPS)
    o_ref[...] = (y * g_ref[...]).astype(o_ref.dtype)


def _first_rmsnorm(front, x, g, nt):
    nb, _, d = x.shape
    tm = ROW_TILE
    return pl.pallas_call(
        functools.partial(_rmsnorm_kernel, nt=nt),
        out_shape=jax.ShapeDtypeStruct((nb * nt * tm, d), BF16),
        grid=(nb * nt,),
        in_specs=[pl.BlockSpec((tm, d), lambda i: (0, 0)),
                  _x_tile_spec(tm, d, nt),
                  pl.BlockSpec((1, d), lambda i: (0, 0))],
        out_specs=pl.BlockSpec((tm, d), lambda i: (i, 0)),
        compiler_params=_params(("arbitrary",)),
        name="rmsnorm",
    )(front, x, g.reshape(1, d))


def _in_proj_kernel(x_ref, wa_ref, wb_ref, cos_ref, sin_ref, u_ref, p_ref, *, bounds, dk, scale):
    j = pl.program_id(1)
    wc = wa_ref.shape[1]

    tm = x_ref.shape[0]
    rh = tm // PROJ_ROW_SPLIT

    def dot_chunk(w_ref, r0, c0):
        return jnp.dot(x_ref[r0:r0 + rh, :], w_ref[:, c0:c0 + dk].astype(BF16), preferred_element_type=F32)

    def store_p(fn):
        for half, w_ref in enumerate((wa_ref, wb_ref)):
            for c0 in range(0, wc, dk):
                for r0 in range(0, tm, rh):
                    p_ref[r0:r0 + rh, half * wc + c0:half * wc + c0 + dk] = fn(
                        dot_chunk(w_ref, r0, c0), r0).astype(p_ref.dtype)

    def rotate(r, r0, mult):
        even = lax.broadcasted_iota(jnp.int32, (r.shape[0], LANES), 1) % 2 == 0
        parts = []
        for l0 in range(0, dk, LANES):
            xl = r[:, l0:l0 + LANES]
            nxt = pltpu.roll(xl, LANES - 1, 1)
            prv = pltpu.roll(xl, 1, 1)
            y = (xl * cos_ref[r0:r0 + rh, l0:l0 + LANES]
                 + jnp.where(even, nxt, prv) * sin_ref[r0:r0 + rh, l0:l0 + LANES])
            parts.append(y if mult is None else y * mult)
        return jnp.concatenate(parts, axis=1)

    phase = lambda name: jnp.logical_and(j >= bounds[name][0], j < bounds[name][1])

    @pl.when(phase("glu"))
    def _():
        for c0 in range(0, wc, dk):
            for r0 in range(0, tm, rh):
                a = dot_chunk(wa_ref, r0, c0)
                gl = dot_chunk(wb_ref, r0, c0)
                u_ref[r0:r0 + rh, c0:c0 + dk] = (a * _sigmoid(gl)).astype(u_ref.dtype)

    @pl.when(jnp.logical_or(phase("conv_gate"), phase("ret_gate")))
    def _():
        store_p(lambda r, r0: _silu(r))

    @pl.when(phase("q"))
    def _():
        store_p(lambda r, r0: rotate(r, r0, scale))

    @pl.when(phase("k"))
    def _():
        store_p(lambda r, r0: rotate(r, r0, None))

    @pl.when(phase("v"))
    def _():
        store_p(lambda r, r0: r)

    @pl.when(jnp.logical_or(phase("merge_a"), phase("merge_b")))
    def _():
        store_p(lambda r, r0: _sigmoid(r))


def _in_proj(hn, w_all, layer, cos2, sin2, tm, widths, dk):
    m, k = hn.shape
    wc = PROJ_COLS
    ntm = cos2.shape[0] // tm
    n_glu = widths["glu_a"] // wc
    assert widths["glu_a"] == widths["glu_b"]
    bounds = {"glu": (0, n_glu)}
    step = n_glu
    for name in ("conv_gate", "q", "k", "v", "ret_gate", "merge_a", "merge_b"):
        n = widths[name] // (2 * wc)
        bounds[name] = (step, step + n)
        step += n
    n_p = step - n_glu
    blk0 = 2 * n_glu
    wa_map = lambda i, j: (layer, 0, jnp.where(j < n_glu, j, blk0 + 2 * (j - n_glu)))
    wb_map = lambda i, j: (layer, 0, jnp.where(j < n_glu, n_glu + j, blk0 + 2 * (j - n_glu) + 1))
    tab = pl.BlockSpec((tm, dk), lambda i, j: (i % ntm, 0))
    return pl.pallas_call(
        functools.partial(_in_proj_kernel, bounds=bounds, dk=dk, scale=dk ** -0.5),
        out_shape=(jax.ShapeDtypeStruct((m, n_glu * wc), BF16),
                   jax.ShapeDtypeStruct((m, n_p * 2 * wc), BF16)),
        grid=(m // tm, step),
        in_specs=[pl.BlockSpec((tm, k), lambda i, j: (i, 0)),
                  pl.BlockSpec((None, k, wc), wa_map),
                  pl.BlockSpec((None, k, wc), wb_map),
                  tab, tab],
        out_specs=(pl.BlockSpec((tm, wc), lambda i, j: (i, jnp.minimum(j, n_glu - 1))),
                   pl.BlockSpec((tm, 2 * wc), lambda i, j: (i, jnp.maximum(j - n_glu, 0)))),
        compiler_params=_params(("arbitrary", "arbitrary")),
        name="in_proj",
    )(hn, w_all, w_all, cos2, sin2)


def _conformer_kernel(u_ref, sgate_ref, sma_ref, cw_ref, cb_ref, lnw_ref, lnb_ref,
                      wproj_ref, o_ref, uext_ref, conv_ref, *, kw):
    t = pl.program_id(1)
    tt = u_ref.shape[0]
    d = u_ref.shape[1]
    ncb = d // LANES
    first = CONV_HIST - (kw - 1)

    @pl.when(t == 0)
    def _():
        uext_ref[:, 0:CONV_HIST, :] = jnp.zeros((ncb, CONV_HIST, LANES), F32)

    @pl.when(t > 0)
    def _():
        uext_ref[:, 0:CONV_HIST, :] = uext_ref[:, tt:tt + CONV_HIST, :]

    for c in range(ncb):
        uext_ref[c, CONV_HIST:CONV_HIST + tt, :] = u_ref[:, c * LANES:(c + 1) * LANES].astype(F32)

    def conv_block(c, carry):
        for rb in range(tt // CONV_ROW_BLOCK):
            r0 = rb * CONV_ROW_BLOCK
            acc = jnp.broadcast_to(cb_ref[c], (CONV_ROW_BLOCK, LANES))
            for k in range(kw):
                acc = acc + cw_ref[c, k:k + 1, :] * uext_ref[c, r0 + first + k:r0 + first + k + CONV_ROW_BLOCK, :]
            conv_ref[c, r0:r0 + CONV_ROW_BLOCK, :] = acc
        return carry

    lax.fori_loop(0, ncb, conv_block, 0)

    cv = conv_ref[...]
    mu = jnp.sum(jnp.sum(cv, axis=0), axis=-1, keepdims=True) / d
    dev = cv - mu[None]
    var = jnp.sum(jnp.sum(dev * dev, axis=0), axis=-1, keepdims=True) / d
    y = dev * lax.rsqrt(var + EPS)[None] * lnw_ref[...] + lnb_ref[...]
    act = _silu(y)
    lhs = jnp.concatenate(
        [(act[c] * sgate_ref[:, c * LANES:(c + 1) * LANES].astype(F32)).astype(BF16) for c in range(ncb)],
        axis=1)
    ya = jnp.dot(lhs, wproj_ref[...], preferred_element_type=F32)
    o_ref[...] = sma_ref[...].astype(F32) * ya


def _conformer(u, p, conv_w, conv_b, ln_w, ln_b, w_proj, nb, nt, pcol):
    m, d = u.shape
    kw = conv_w.shape[0]
    ncb = d // LANES
    tt = ROW_TILE
    cw = conv_w.reshape(kw, ncb, LANES).transpose(1, 0, 2)
    blk = lambda col: pl.BlockSpec((tt, d), lambda b, t, col=col: (b * nt + t, col))
    vec3 = pl.BlockSpec((ncb, 1, LANES), lambda b, t: (0, 0, 0))
    return pl.pallas_call(
        functools.partial(_conformer_kernel, kw=kw),
        out_shape=jax.ShapeDtypeStruct((m, d), F32),
        grid=(nb, nt),
        in_specs=[blk(0), blk(pcol["conv_gate"] // d), blk(pcol["merge_a"] // d),
                  pl.BlockSpec((ncb, kw, LANES), lambda b, t: (0, 0, 0)),
                  vec3, vec3, vec3,
                  pl.BlockSpec((d, d), lambda b, t: (0, 0), pipeline_mode=pl.Buffered(1))],
        out_specs=pl.BlockSpec((tt, d), lambda b, t: (b * nt + t, 0)),
        scratch_shapes=[pltpu.VMEM((ncb, CONV_HIST + tt, LANES), F32),
                        pltpu.VMEM((ncb, tt, LANES), F32)],
        compiler_params=_params(("parallel", "arbitrary")),
        name="conformer",
    )(u, p, p, cw, conv_b.reshape(ncb, 1, LANES), ln_w.reshape(ncb, 1, LANES),
      ln_b.reshape(ncb, 1, LANES), w_proj)


def _retention_kernel(logg_ref, q_ref, k_ref, v_ref, sg_ref, gnw_ref, gnb_ref,
                      o_ref, state_ref, decay_ref, xi_ref, zeta_ref, *, chunk):
    t = pl.program_id(0)
    b = pl.program_id(1)
    h = pl.program_id(2)
    nsub = q_ref.shape[0] // chunk
    lg = logg_ref[h]
    slot = b * pl.num_programs(2) + h

    @pl.when(jnp.logical_and(t == 0, b == 0))
    def _():
        i = lax.broadcasted_iota(jnp.int32, (chunk, chunk), 0)
        j = lax.broadcasted_iota(jnp.int32, (chunk, chunk), 1)
        dist = (i - j).astype(F32)
        decay_ref[h] = jnp.where(dist >= 0, jnp.exp(lg * jnp.maximum(dist, 0.0)), 0.0)
        idx = lax.broadcasted_iota(jnp.int32, (chunk, 1), 0).astype(F32)
        xi_ref[h] = jnp.exp(lg * (idx + 1.0))
        zeta_ref[h] = jnp.exp(lg * (chunk - 1.0 - idx))

    @pl.when(t == 0)
    def _():
        state_ref[slot] = jnp.zeros(state_ref.shape[1:], F32)

    g_chunk = jnp.exp(jnp.full((1, 1), lg * chunk, F32))

    def step(c, carry):
        rows = pl.ds(pl.multiple_of(c * chunk, chunk), chunk)
        qb = q_ref[rows, :]
        kb = k_ref[rows, :]
        v = v_ref[rows, :]
        state = state_ref[slot]
        s = lax.dot_general(qb, kb, (((1,), (1,)), ((), ())), preferred_element_type=F32) * decay_ref[h]
        inner = jnp.dot(s.astype(BF16), v, preferred_element_type=F32)
        cross = jnp.dot(qb, state.astype(BF16), preferred_element_type=F32) * xi_ref[h]
        kz = (kb.astype(F32) * zeta_ref[h]).astype(BF16)
        state_ref[slot] = state * g_chunk + lax.dot_general(
            kz, v, (((0,), (0,)), ((), ())), preferred_element_type=F32)
        o = inner + cross
        mu = jnp.mean(o, axis=-1, keepdims=True)
        dev = o - mu
        var = jnp.mean(dev * dev, axis=-1, keepdims=True)
        y = dev * lax.rsqrt(var + EPS) * gnw_ref[...] + gnb_ref[...]
        o_ref[rows, :] = (y * sg_ref[rows, :].astype(F32)).astype(o_ref.dtype)
        return carry

    lax.fori_loop(0, nsub, step, 0, unroll=3)


def _retention(p, log_g, gn_w, gn_b, nb, nt, tr, pcol, dk, dv):
    m = p.shape[0]
    heads = RET_HEADS
    chunk = ROW_TILE
    qblk = lambda col: pl.BlockSpec((tr, dk), lambda t, b, h, col=col: (b * nt + t, col + h))
    vblk = lambda col: pl.BlockSpec((tr, dv), lambda t, b, h, col=col: (b * nt + t, col + h))
    vec = pl.BlockSpec((1, dv), lambda t, b, h: (0, h))
    return pl.pallas_call(
        functools.partial(_retention_kernel, chunk=chunk),
        out_shape=jax.ShapeDtypeStruct((m, heads * dv), BF16),
        grid=(nt, nb, heads),
        in_specs=[pl.BlockSpec(memory_space=pltpu.SMEM),
                  qblk(pcol["q"] // dk), qblk(pcol["k"] // dk),
                  vblk(pcol["v"] // dv), vblk(pcol["ret_gate"] // dv),
                  vec, vec],
        out_specs=pl.BlockSpec((tr, dv), lambda t, b, h: (b * nt + t, h)),
        scratch_shapes=[pltpu.VMEM((nb * heads, dk, dv), F32),
                        pltpu.VMEM((heads, chunk, chunk), F32),
                        pltpu.VMEM((heads, chunk, 1), F32),
                        pltpu.VMEM((heads, chunk, 1), F32)],
        compiler_params=_params(("arbitrary", "arbitrary", "arbitrary")),
        name="retention",
    )(log_g, p, p, p, p, gn_w.reshape(1, -1), gn_b.reshape(1, -1))


def _tail_kernel(act_ref, wret_ref, smb_ref, ma_ref, wout_ref, g_ref, *refs, nt, pad_rows, from_x, last):
    i = pl.program_id(0)
    t = i % nt
    n_in = 2 if from_x else 1
    resid = _front_or_x(t, *refs[:n_in]) if from_x else refs[0][...]
    out_refs = refs[n_in:]
    tm = resid.shape[0]
    yb = jnp.dot(act_ref[...], wret_ref[...], preferred_element_type=F32)
    merged = (ma_ref[...] + smb_ref[...].astype(F32) * yb).astype(BF16)
    hn = resid + jnp.dot(merged, wout_ref[...], preferred_element_type=F32)
    row = t * tm + lax.broadcasted_iota(jnp.int32, (tm, 1), 0)
    hn = jnp.where(row >= pad_rows, hn, 0.0)
    y = hn * lax.rsqrt(jnp.mean(hn * hn, axis=-1, keepdims=True) + EPS)
    if not last:
        out_refs[0][...] = hn
    out_refs[-1][...] = (y * g_ref[...]).astype(out_refs[-1].dtype)


def _tail(act_b, w_ret, p, col_mb, m_a, w_out, g_next, resid, nb, nt, pad_rows, from_x, last, out_dtype):
    m, d = m_a.shape
    kdim = act_b.shape[1]
    tm = ROW_TILE
    row = pl.BlockSpec((tm, d), lambda i: (i, 0))
    whole = lambda r, c: pl.BlockSpec((r, c), lambda i: (0, 0), pipeline_mode=pl.Buffered(1))
    resid_specs = [pl.BlockSpec((tm, d), lambda i: (0, 0)), _x_tile_spec(tm, d, nt)] if from_x else [row]
    if last:
        out_shape = jax.ShapeDtypeStruct((nb, (nt - 1) * tm, d), out_dtype)
        out_specs = _x_tile_spec(tm, d, nt)
    else:
        out_shape = (jax.ShapeDtypeStruct((m, d), F32), jax.ShapeDtypeStruct((m, d), BF16))
        out_specs = (row, row)
    return pl.pallas_call(
        functools.partial(_tail_kernel, nt=nt, pad_rows=pad_rows, from_x=from_x, last=last),
        out_shape=out_shape,
        grid=(m // tm,),
        in_specs=[pl.BlockSpec((tm, kdim), lambda i: (i, 0)),
                  whole(kdim, d),
                  pl.BlockSpec((tm, d), lambda i: (i, col_mb)),
                  row,
                  whole(d, d),
                  pl.BlockSpec((1, d), lambda i: (0, 0))] + resid_specs,
        out_specs=out_specs,
        compiler_params=_params(("arbitrary",)),
        name="tail",
    )(act_b, w_ret, p, m_a, w_out, g_next.reshape(1, d), *resid)


def kernel(x, meta_tokens, norm_g, w_in, conv_w, conv_b, conv_ln_w, conv_ln_b, conv_out,
           ret_gn_w, ret_gn_b, ret_out, w_out, final_norm_g):
    nb, seq, d = x.shape
    depth = w_in.shape[0]
    n_meta = meta_tokens.shape[0]
    heads = RET_HEADS
    conv_width = conv_out.shape[1]
    ret_width = ret_out.shape[1]
    dk = d // heads
    dv = ret_width // heads
    qk_width = heads * dk
    assert seq % ROW_TILE == 0 and n_meta <= FRONT_ROWS and conv_width == d

    widths = {"glu_a": conv_width, "glu_b": conv_width, "conv_gate": conv_width, "q": qk_width, "k": qk_width,
              "v": ret_width, "ret_gate": ret_width, "merge_a": d, "merge_b": d}
    assert sum(widths.values()) == w_in.shape[2]
    pcol = {}
    off = 0
    for name in ("conv_gate", "q", "k", "v", "ret_gate", "merge_a", "merge_b"):
        pcol[name] = off
        off += widths[name]

    pad_rows = FRONT_ROWS - n_meta
    lp = FRONT_ROWS + seq
    nt = lp // ROW_TILE
    m = nb * lp

    front = jnp.concatenate([jnp.zeros((pad_rows, d), x.dtype), meta_tokens.astype(x.dtype)], axis=0)

    pos = (jnp.arange(lp, dtype=jnp.int32) - pad_rows).astype(F32)
    inv_freq = 1.0 / (ROPE_BASE ** jnp.linspace(0.0, 1.0, dk // 2, dtype=F32))
    ang2 = pos[:, None] * jnp.repeat(inv_freq, 2)[None]
    cos2 = jnp.cos(ang2)
    sin2 = jnp.sin(ang2) * jnp.tile(jnp.array([-1.0, 1.0], F32), dk // 2)[None]
    log_g = jnp.log(1.0 - jnp.exp2(-5.0 - jnp.arange(heads, dtype=F32)))

    tm_proj = 1280 if lp % 1280 == 0 else ROW_TILE
    tr = next(c for c in (3328, 1280, ROW_TILE) if lp % c == 0)

    hn = _first_rmsnorm(front, x, norm_g[0], nt)
    h = None
    for l in range(depth):
        u, p = _in_proj(hn, w_in, l, cos2, sin2, tm_proj, widths, dk)
        m_a = _conformer(u, p, conv_w[l], conv_b[l], conv_ln_w[l], conv_ln_b[l],
                         conv_out[l].astype(BF16), nb, nt, pcol)
        act_b = _retention(p, log_g, ret_gn_w[l], ret_gn_b[l], nb, lp // tr, tr, pcol, dk, dv)
        last = l == depth - 1
        g_next = final_norm_g if last else norm_g[l + 1]
        resid = (front, x) if l == 0 else (h,)
        res = _tail(act_b, ret_out[l].astype(BF16), p, pcol["merge_b"] // d, m_a, w_out[l].astype(BF16),
                    g_next, resid, nb, nt, pad_rows, from_x=(l == 0), last=last, out_dtype=x.dtype)
        if last:
            return res
        h, hn = res
```

```python
import functools

import jax
import jax.numpy as jnp
from jax import lax
from jax.experimental import pallas as pl
from jax.experimental.pallas import tpu as pltpu

F32 = jnp.float32
BF16 = jnp.bfloat16

LANES = 128
ROW_TILE = 256
FRONT_ROWS = ROW_TILE
PROJ_COLS = 512
PROJ_ROW_SPLIT = 4
CONV_HIST = 32
CONV_ROW_BLOCK = 64
RET_HEADS = 8
ROPE_BASE = 10000.0
EPS = 1e-6
VMEM_LIMIT = 56 * 1024 * 1024


def _sigmoid(v):
    return 0.5 * jnp.tanh(0.5 * v) + 0.5


def _silu(v):
    return v * _sigmoid(v)


def _params(sem, vmem=VMEM_LIMIT):
    return pltpu.CompilerParams(dimension_semantics=sem, vmem_limit_bytes=vmem)


def _front_or_x(t, front_ref, x_ref):
    return jnp.where(t == 0, front_ref[...], x_ref[...])


def _x_tile_spec(tm, d, nt):
    return pl.BlockSpec((None, tm, d), lambda i: (i // nt, jnp.maximum(i % nt - 1, 0), 0))


def _rmsnorm_kernel(front_ref, x_ref, g_ref, o_ref, *, nt):
    x = _front_or_x(pl.program_id(0) % nt, front_ref, x_ref)
    y = x * lax.rsqrt(jnp.mean(x * x, axis=-1, keepdims=True) + EPS)
    o_ref[...] = (y * g_ref[...]).astype(o_ref.dtype)


def _first_rmsnorm(front, x, g, nt):
    nb, _, d = x.shape
    tm = ROW_TILE
    return pl.pallas_call(
        functools.partial(_rmsnorm_kernel, nt=nt),
        out_shape=jax.ShapeDtypeStruct((nb * nt * tm, d), BF16),
        grid=(nb * nt,),
        in_specs=[pl.BlockSpec((tm, d), lambda i: (0, 0)),
                  _x_tile_spec(tm, d, nt),
                  pl.BlockSpec((1, d), lambda i: (0, 0))],
        out_specs=pl.BlockSpec((tm, d), lambda i: (i, 0)),
        compiler_params=_params(("arbitrary",)),
        name="rmsnorm",
    )(front, x, g.reshape(1, d))


def _in_proj_kernel(x_ref, wa_ref, wb_ref, cos_ref, sin_ref, u_ref, p_ref, *, bounds, dk, scale):
    j = pl.program_id(1)
    wc = wa_ref.shape[1]

    tm = x_ref.shape[0]
    rh = tm // PROJ_ROW_SPLIT

    def dot_chunk(w_ref, r0, c0):
        return jnp.dot(x_ref[r0:r0 + rh, :], w_ref[:, c0:c0 + dk].astype(BF16), preferred_element_type=F32)

    def store_p(fn):
        for half, w_ref in enumerate((wa_ref, wb_ref)):
            for c0 in range(0, wc, dk):
                for r0 in range(0, tm, rh):
                    p_ref[r0:r0 + rh, half * wc + c0:half * wc + c0 + dk] = fn(
                        dot_chunk(w_ref, r0, c0), r0).astype(p_ref.dtype)

    def rotate(r, r0, mult):
        even = lax.broadcasted_iota(jnp.int32, (r.shape[0], LANES), 1) % 2 == 0
        parts = []
        for l0 in range(0, dk, LANES):
            xl = r[:, l0:l0 + LANES]
            nxt = pltpu.roll(xl, LANES - 1, 1)
            prv = pltpu.roll(xl, 1, 1)
            y = (xl * cos_ref[r0:r0 + rh, l0:l0 + LANES]
                 + jnp.where(even, nxt, prv) * sin_ref[r0:r0 + rh, l0:l0 + LANES])
            parts.append(y if mult is None else y * mult)
        return jnp.concatenate(parts, axis=1)

    phase = lambda name: jnp.logical_and(j >= bounds[name][0], j < bounds[name][1])

    @pl.when(phase("glu"))
    def _():
        for c0 in range(0, wc, dk):
            for r0 in range(0, tm, rh):
                a = dot_chunk(wa_ref, r0, c0)
                gl = dot_chunk(wb_ref, r0, c0)
                u_ref[r0:r0 + rh, c0:c0 + dk] = (a * _sigmoid(gl)).astype(u_ref.dtype)

    @pl.when(jnp.logical_or(phase("conv_gate"), phase("ret_gate")))
    def _():
        store_p(lambda r, r0: _silu(r))

    @pl.when(phase("q"))
    def _():
        store_p(lambda r, r0: rotate(r, r0, scale))

    @pl.when(phase("k"))
    def _():
        store_p(lambda r, r0: rotate(r, r0, None))

    @pl.when(phase("v"))
    def _():
        store_p(lambda r, r0: r)

    @pl.when(jnp.logical_or(phase("merge_a"), phase("merge_b")))
    def _():
        store_p(lambda r, r0: _sigmoid(r))


def _in_proj(hn, w_all, layer, cos2, sin2, tm, widths, dk):
    m, k = hn.shape
    wc = PROJ_COLS
    ntm = cos2.shape[0] // tm
    n_glu = widths["glu_a"] // wc
    assert widths["glu_a"] == widths["glu_b"]
    bounds = {"glu": (0, n_glu)}
    step = n_glu
    for name in ("conv_gate", "q", "k", "v", "ret_gate", "merge_a", "merge_b"):
        n = widths[name] // (2 * wc)
        bounds[name] = (step, step + n)
        step += n
    n_p = step - n_glu
    blk0 = 2 * n_glu
    wa_map = lambda i, j: (layer, 0, jnp.where(j < n_glu, j, blk0 + 2 * (j - n_glu)))
    wb_map = lambda i, j: (layer, 0, jnp.where(j < n_glu, n_glu + j, blk0 + 2 * (j - n_glu) + 1))
    tab = pl.BlockSpec((tm, dk), lambda i, j: (i % ntm, 0))
    return pl.pallas_call(
        functools.partial(_in_proj_kernel, bounds=bounds, dk=dk, scale=dk ** -0.5),
        out_shape=(jax.ShapeDtypeStruct((m, n_glu * wc), BF16),
                   jax.ShapeDtypeStruct((m, n_p * 2 * wc), BF16)),
        grid=(m // tm, step),
        in_specs=[pl.BlockSpec((tm, k), lambda i, j: (i, 0)),
                  pl.BlockSpec((None, k, wc), wa_map),
                  pl.BlockSpec((None, k, wc), wb_map),
                  tab, tab],
        out_specs=(pl.BlockSpec((tm, wc), lambda i, j: (i, jnp.minimum(j, n_glu - 1))),
                   pl.BlockSpec((tm, 2 * wc), lambda i, j: (i, jnp.maximum(j - n_glu, 0)))),
        compiler_params=_params(("arbitrary", "arbitrary")),
        name="in_proj",
    )(hn, w_all, w_all, cos2, sin2)


def _conformer_kernel(u_ref, sgate_ref, sma_ref, cw_ref, cb_ref, lnw_ref, lnb_ref,
                      wproj_ref, o_ref, uext_ref, conv_ref, act_ref, ya_ref, *, kw):
    b = pl.program_id(0)
    t = pl.program_id(1)
    tt = u_ref.shape[0]
    d = u_ref.shape[1]
    ncb = d // LANES
    nchunk = wproj_ref.shape[0]
    cpb = ncb // nchunk
    wn = wproj_ref.shape[2]
    first = CONV_HIST - (kw - 1)

    @pl.when(jnp.logical_and(b == 0, t == 0))
    def _():
        act_ref[...] = jnp.zeros(act_ref.shape, act_ref.dtype)

    @pl.when(t == 0)
    def _():
        uext_ref[:, 0:CONV_HIST, :] = jnp.zeros((ncb, CONV_HIST, LANES), F32)

    @pl.when(t > 0)
    def _():
        uext_ref[:, 0:CONV_HIST, :] = uext_ref[:, tt:tt + CONV_HIST, :]

    for c in range(ncb):
        uext_ref[c, CONV_HIST:CONV_HIST + tt, :] = u_ref[:, c * LANES:(c + 1) * LANES].astype(F32)

    def fused_block(n, carry):
        for q in range(cpb):
            c = n * cpb + q
            for rb in range(tt // CONV_ROW_BLOCK):
                r0 = rb * CONV_ROW_BLOCK
                acc = jnp.broadcast_to(cb_ref[c], (CONV_ROW_BLOCK, LANES))
                for k in range(kw):
                    acc = acc + cw_ref[c, k:k + 1, :] * uext_ref[c, r0 + first + k:r0 + first + k + CONV_ROW_BLOCK, :]
                conv_ref[c, r0:r0 + CONV_ROW_BLOCK, :] = acc
        ya_ref[n] = jnp.dot(act_ref[...], wproj_ref[n], preferred_element_type=F32)
        return carry

    lax.fori_loop(0, nchunk, fused_block, 0)

    for n in range(nchunk):
        o_ref[:, n * wn:(n + 1) * wn] = sma_ref[:, n * wn:(n + 1) * wn].astype(F32) * ya_ref[n]

    def norm_rows(rb, carry):
        rows = pl.ds(pl.multiple_of(rb * CONV_ROW_BLOCK, CONV_ROW_BLOCK), CONV_ROW_BLOCK)
        tot = conv_ref[0, rows, :]
        for c in range(1, ncb):
            tot = tot + conv_ref[c, rows, :]
        mu = jnp.sum(tot, axis=-1, keepdims=True) / d
        sq = jnp.zeros_like(tot)
        for c in range(ncb):
            dev = conv_ref[c, rows, :] - mu
            sq = sq + dev * dev
        rstd = lax.rsqrt(jnp.sum(sq, axis=-1, keepdims=True) / d + EPS)
        for c in range(ncb):
            lanes = slice(c * LANES, (c + 1) * LANES)
            y = (conv_ref[c, rows, :] - mu) * rstd * lnw_ref[c] + lnb_ref[c]
            act_ref[rows, lanes] = (_silu(y) * sgate_ref[rows, lanes].astype(F32)).astype(BF16)
        return carry

    lax.fori_loop(0, tt // CONV_ROW_BLOCK, norm_rows, 0)


def _conformer(u, p, conv_w, conv_b, ln_w, ln_b, w_proj, nb, nt, pcol):
    m, d = u.shape
    kw = conv_w.shape[0]
    ncb = d // LANES
    tt = ROW_TILE
    wn = 2 * LANES
    nchunk = d // wn
    cw = conv_w.reshape(kw, ncb, LANES).transpose(1, 0, 2)
    wp = w_proj.reshape(d, nchunk, wn).transpose(1, 0, 2)
    cur = lambda col: pl.BlockSpec((tt, d), lambda b, t, col=col: (b * nt + jnp.minimum(t, nt - 1), col))
    prev = lambda col: pl.BlockSpec((tt, d), lambda b, t, col=col: (b * nt + jnp.maximum(t - 1, 0), col))
    vec3 = pl.BlockSpec((ncb, 1, LANES), lambda b, t: (0, 0, 0))
    return pl.pallas_call(
        functools.partial(_conformer_kernel, kw=kw),
        out_shape=jax.ShapeDtypeStruct((m, d), F32),
        grid=(nb, nt + 1),
        in_specs=[cur(0), cur(pcol["conv_gate"] // d), prev(pcol["merge_a"] // d),
                  pl.BlockSpec((ncb, kw, LANES), lambda b, t: (0, 0, 0)),
                  vec3, vec3, vec3,
                  pl.BlockSpec((nchunk, d, wn), lambda b, t: (0, 0, 0), pipeline_mode=pl.Buffered(1))],
        out_specs=prev(0),
        scratch_shapes=[pltpu.VMEM((ncb, CONV_HIST + tt, LANES), F32),
                        pltpu.VMEM((ncb, tt, LANES), F32),
                        pltpu.VMEM((tt, d), BF16),
                        pltpu.VMEM((nchunk, tt, wn), F32)],
        compiler_params=_params(("arbitrary", "arbitrary")),
        name="conformer",
    )(u, p, p, cw, conv_b.reshape(ncb, 1, LANES), ln_w.reshape(ncb, 1, LANES),
      ln_b.reshape(ncb, 1, LANES), wp)


def _retention_kernel(logg_ref, q_ref, k_ref, v_ref, sg_ref, gnw_ref, gnb_ref,
                      o_ref, state_ref, decay_ref, xi_ref, zeta_ref, *, chunk):
    t = pl.program_id(0)
    b = pl.program_id(1)
    h = pl.program_id(2)
    nsub = q_ref.shape[0] // chunk
    lg = logg_ref[h]
    slot = b * pl.num_programs(2) + h

    @pl.when(jnp.logical_and(t == 0, b == 0))
    def _():
        i = lax.broadcasted_iota(jnp.int32, (chunk, chunk), 0)
        j = lax.broadcasted_iota(jnp.int32, (chunk, chunk), 1)
        dist = (i - j).astype(F32)
        decay_ref[h] = jnp.where(dist >= 0, jnp.exp(lg * jnp.maximum(dist, 0.0)), 0.0)
        idx = lax.broadcasted_iota(jnp.int32, (chunk, 1), 0).astype(F32)
        xi_ref[h] = jnp.exp(lg * (idx + 1.0))
        zeta_ref[h] = jnp.exp(lg * (chunk - 1.0 - idx))

    @pl.when(t == 0)
    def _():
        state_ref[slot] = jnp.zeros(state_ref.shape[1:], F32)

    g_chunk = jnp.exp(jnp.full((1, 1), lg * chunk, F32))

    def step(c, carry):
        rows = pl.ds(pl.multiple_of(c * chunk, chunk), chunk)
        qb = q_ref[rows, :]
        kb = k_ref[rows, :]
        v = v_ref[rows, :]
        state = state_ref[slot]
        s = lax.dot_general(qb, kb, (((1,), (1,)), ((), ())), preferred_element_type=F32) * decay_ref[h]
        inner = jnp.dot(s.astype(BF16), v, preferred_element_type=F32)
        cross = jnp.dot(qb, state.astype(BF16), preferred_element_type=F32) * xi_ref[h]
        kz = (kb.astype(F32) * zeta_ref[h]).astype(BF16)
        state_ref[slot] = state * g_chunk + lax.dot_general(
            kz, v, (((0,), (0,)), ((), ())), preferred_element_type=F32)
        o = inner + cross
        mu = jnp.mean(o, axis=-1, keepdims=True)
        dev = o - mu
        var = jnp.mean(dev * dev, axis=-1, keepdims=True)
        y = dev * lax.rsqrt(var + EPS) * gnw_ref[...] + gnb_ref[...]
        o_ref[rows, :] = (y * sg_ref[rows, :].astype(F32)).astype(o_ref.dtype)
        return carry

    lax.fori_loop(0, nsub, step, 0, unroll=3)


def _retention(p, log_g, gn_w, gn_b, nb, nt, tr, pcol, dk, dv):
    m = p.shape[0]
    heads = RET_HEADS
    chunk = ROW_TILE
    qblk = lambda col: pl.BlockSpec((tr, dk), lambda t, b, h, col=col: (b * nt + t, col + h))
    vblk = lambda col: pl.BlockSpec((tr, dv), lambda t, b, h, col=col: (b * nt + t, col + h))
    vec = pl.BlockSpec((1, dv), lambda t, b, h: (0, h))
    return pl.pallas_call(
        functools.partial(_retention_kernel, chunk=chunk),
        out_shape=jax.ShapeDtypeStruct((m, heads * dv), BF16),
        grid=(nt, nb, heads),
        in_specs=[pl.BlockSpec(memory_space=pltpu.SMEM),
                  qblk(pcol["q"] // dk), qblk(pcol["k"] // dk),
                  vblk(pcol["v"] // dv), vblk(pcol["ret_gate"] // dv),
                  vec, vec],
        out_specs=pl.BlockSpec((tr, dv), lambda t, b, h: (b * nt + t, h)),
        scratch_shapes=[pltpu.VMEM((nb * heads, dk, dv), F32),
                        pltpu.VMEM((heads, chunk, chunk), F32),
                        pltpu.VMEM((heads, chunk, 1), F32),
                        pltpu.VMEM((heads, chunk, 1), F32)],
        compiler_params=_params(("arbitrary", "arbitrary", "arbitrary")),
        name="retention",
    )(log_g, p, p, p, p, gn_w.reshape(1, -1), gn_b.reshape(1, -1))


def _tail_kernel(act_ref, wret_ref, smb_ref, ma_ref, wout_ref, g_ref, *refs, nt, pad_rows, from_x, last):
    i = pl.program_id(0)
    t = i % nt
    n_in = 2 if from_x else 1
    resid = _front_or_x(t, *refs[:n_in]) if from_x else refs[0][...]
    out_refs = refs[n_in:]
    tm = resid.shape[0]
    yb = jnp.dot(act_ref[...], wret_ref[...], preferred_element_type=F32)
    merged = (ma_ref[...] + smb_ref[...].astype(F32) * yb).astype(BF16)
    hn = resid + jnp.dot(merged, wout_ref[...], preferred_element_type=F32)
    row = t * tm + lax.broadcasted_iota(jnp.int32, (tm, 1), 0)
    hn = jnp.where(row >= pad_rows, hn, 0.0)
    y = hn * lax.rsqrt(jnp.mean(hn * hn, axis=-1, keepdims=True) + EPS)
    if not last:
        out_refs[0][...] = hn
    out_refs[-1][...] = (y * g_ref[...]).astype(out_refs[-1].dtype)


def _tail(act_b, w_ret, p, col_mb, m_a, w_out, g_next, resid, nb, nt, pad_rows, from_x, last, out_dtype):
    m, d = m_a.shape
    kdim = act_b.shape[1]
    tm = ROW_TILE
    row = pl.BlockSpec((tm, d), lambda i: (i, 0))
    whole = lambda r, c: pl.BlockSpec((r, c), lambda i: (0, 0), pipeline_mode=pl.Buffered(1))
    resid_specs = [pl.BlockSpec((tm, d), lambda i: (0, 0)), _x_tile_spec(tm, d, nt)] if from_x else [row]
    if last:
        out_shape = jax.ShapeDtypeStruct((nb, (nt - 1) * tm, d), out_dtype)
        out_specs = _x_tile_spec(tm, d, nt)
    else:
        out_shape = (jax.ShapeDtypeStruct((m, d), F32), jax.ShapeDtypeStruct((m, d), BF16))
        out_specs = (row, row)
    return pl.pallas_call(
        functools.partial(_tail_kernel, nt=nt, pad_rows=pad_rows, from_x=from_x, last=last),
        out_shape=out_shape,
        grid=(m // tm,),
        in_specs=[pl.BlockSpec((tm, kdim), lambda i: (i, 0)),
                  whole(kdim, d),
                  pl.BlockSpec((tm, d), lambda i: (i, col_mb)),
                  row,
                  whole(d, d),
                  pl.BlockSpec((1, d), lambda i: (0, 0))] + resid_specs,
        out_specs=out_specs,
        compiler_params=_params(("arbitrary",)),
        name="tail",
    )(act_b, w_ret, p, m_a, w_out, g_next.reshape(1, d), *resid)


def kernel(x, meta_tokens, norm_g, w_in, conv_w, conv_b, conv_ln_w, conv_ln_b, conv_out,
           ret_gn_w, ret_gn_b, ret_out, w_out, final_norm_g):
    nb, seq, d = x.shape
    depth = w_in.shape[0]
    n_meta = meta_tokens.shape[0]
    heads = RET_HEADS
    conv_width = conv_out.shape[1]
    ret_width = ret_out.shape[1]
    dk = d // heads
    dv = ret_width // heads
    qk_width = heads * dk
    assert seq % ROW_TILE == 0 and n_meta <= FRONT_ROWS and conv_width == d

    widths = {"glu_a": conv_width, "glu_b": conv_width, "conv_gate": conv_width, "q": qk_width, "k": qk_width,
              "v": ret_width, "ret_gate": ret_width, "merge_a": d, "merge_b": d}
    assert sum(widths.values()) == w_in.shape[2]
    pcol = {}
    off = 0
    for name in ("conv_gate", "q", "k", "v", "ret_gate", "merge_a", "merge_b"):
        pcol[name] = off
        off += widths[name]

    pad_rows = FRONT_ROWS - n_meta
    lp = FRONT_ROWS + seq
    nt = lp // ROW_TILE
    m = nb * lp

    front = jnp.concatenate([jnp.zeros((pad_rows, d), x.dtype), meta_tokens.astype(x.dtype)], axis=0)

    pos = (jnp.arange(lp, dtype=jnp.int32) - pad_rows).astype(F32)
    inv_freq = 1.0 / (ROPE_BASE ** jnp.linspace(0.0, 1.0, dk // 2, dtype=F32))
    ang2 = pos[:, None] * jnp.repeat(inv_freq, 2)[None]
    cos2 = jnp.cos(ang2)
    sin2 = jnp.sin(ang2) * jnp.tile(jnp.array([-1.0, 1.0], F32), dk // 2)[None]
    log_g = jnp.log(1.0 - jnp.exp2(-5.0 - jnp.arange(heads, dtype=F32)))

    tm_proj = 1280 if lp % 1280 == 0 else ROW_TILE
    tr = next(c for c in (3328, 1280, ROW_TILE) if lp % c == 0)

    hn = _first_rmsnorm(front, x, norm_g[0], nt)
    h = None
    for l in range(depth):
        u, p = _in_proj(hn, w_in, l, cos2, sin2, tm_proj, widths, dk)
        m_a = _conformer(u, p, conv_w[l], conv_b[l], conv_ln_w[l], conv_ln_b[l],
                         conv_out[l].astype(BF16), nb, nt, pcol)
        act_b = _retention(p, log_g, ret_gn_w[l], ret_gn_b[l], nb, lp // tr, tr, pcol, dk, dv)
        last = l == depth - 1
        g_next = final_norm_g if last else norm_g[l + 1]
        resid = (front, x) if l == 0 else (h,)
        res = _tail(act_b, ret_out[l].astype(BF16), p, pcol["merge_b"] // d, m_a, w_out[l].astype(BF16),
                    g_next, resid, nb, nt, pad_rows, from_x=(l == 0), last=last, out_dtype=x.dtype)
        if last:
            return res
        h, hn = res
```

```python
import functools

import jax
import jax.numpy as jnp
from jax import lax
from jax.experimental import pallas as pl
from jax.experimental.pallas import tpu as pltpu

F32 = jnp.float32
BF16 = jnp.bfloat16

LANES = 128
ROW_TILE = 256
FRONT_ROWS = ROW_TILE
PROJ_COLS = 512
PROJ_ROW_SPLIT = 4
CONV_HIST = 32
CONV_ROW_BLOCK = 64
RET_HEADS = 8
ROPE_BASE = 10000.0
EPS = 1e-6
VMEM_LIMIT = 56 * 1024 * 1024


def _sigmoid(v):
    return 0.5 * jnp.tanh(0.5 * v) + 0.5


def _silu(v):
    h = 0.5 * v
    return h * jnp.tanh(h) + h


def _params(sem, vmem=VMEM_LIMIT):
    return pltpu.CompilerParams(dimension_semantics=sem, vmem_limit_bytes=vmem)


def _front_or_x(t, front_ref, x_ref):
    return jnp.where(t == 0, front_ref[...], x_ref[...])


def _x_tile_spec(tm, d, nt):
    return pl.BlockSpec((None, tm, d), lambda i: (i // nt, jnp.maximum(i % nt - 1, 0), 0))


def _rmsnorm_kernel(front_ref, x_ref, g_ref, o_ref, *, nt):
    x = _front_or_x(pl.program_id(0) % nt, front_ref, x_ref)
    y = x * lax.rsqrt(jnp.mean(x * x, axis=-1, keepdims=True) + EPS)
    o_ref[...] = (y * g_ref[...]).astype(o_ref.dtype)


def _first_rmsnorm(front, x, g, nt):
    nb, _, d = x.shape
    tm = ROW_TILE
    return pl.pallas_call(
        functools.partial(_rmsnorm_kernel, nt=nt),
        out_shape=jax.ShapeDtypeStruct((nb * nt * tm, d), BF16),
        grid=(nb * nt,),
        in_specs=[pl.BlockSpec((tm, d), lambda i: (0, 0)),
                  _x_tile_spec(tm, d, nt),
                  pl.BlockSpec((1, d), lambda i: (0, 0))],
        out_specs=pl.BlockSpec((tm, d), lambda i: (i, 0)),
        compiler_params=_params(("arbitrary",)),
        name="rmsnorm",
    )(front, x, g.reshape(1, d))


def _in_proj_kernel(x_ref, wa_ref, wb_ref, cos_ref, sin_ref, u_ref, p_ref, *, bounds, dk, scale):
    j = pl.program_id(1)
    wc = wa_ref.shape[1]

    tm = x_ref.shape[0]
    rh = tm // PROJ_ROW_SPLIT

    def dot_chunk(w_ref, r0, c0):
        return jnp.dot(x_ref[r0:r0 + rh, :], w_ref[:, c0:c0 + dk].astype(BF16), preferred_element_type=F32)

    def store_p(fn):
        for half, w_ref in enumerate((wa_ref, wb_ref)):
            for c0 in range(0, wc, dk):
                for r0 in range(0, tm, rh):
                    p_ref[r0:r0 + rh, half * wc + c0:half * wc + c0 + dk] = fn(
                        dot_chunk(w_ref, r0, c0), r0).astype(p_ref.dtype)

    def rotate(r, r0, mult):
        even = lax.broadcasted_iota(jnp.int32, (r.shape[0], LANES), 1) % 2 == 0
        parts = []
        for l0 in range(0, dk, LANES):
            xl = r[:, l0:l0 + LANES]
            nxt = pltpu.roll(xl, LANES - 1, 1)
            prv = pltpu.roll(xl, 1, 1)
            y = (xl * cos_ref[r0:r0 + rh, l0:l0 + LANES]
                 + jnp.where(even, nxt, prv) * sin_ref[r0:r0 + rh, l0:l0 + LANES])
            parts.append(y if mult is None else y * mult)
        return jnp.concatenate(parts, axis=1)

    phase = lambda name: jnp.logical_and(j >= bounds[name][0], j < bounds[name][1])

    @pl.when(phase("glu"))
    def _():
        for c0 in range(0, wc, dk):
            for r0 in range(0, tm, rh):
                a = dot_chunk(wa_ref, r0, c0)
                gl = dot_chunk(wb_ref, r0, c0)
                u_ref[r0:r0 + rh, c0:c0 + dk] = (a * _sigmoid(gl)).astype(u_ref.dtype)

    @pl.when(jnp.logical_or(phase("conv_gate"), phase("ret_gate")))
    def _():
        store_p(lambda r, r0: _silu(r))

    @pl.when(phase("q"))
    def _():
        store_p(lambda r, r0: rotate(r, r0, scale))

    @pl.when(phase("k"))
    def _():
        store_p(lambda r, r0: rotate(r, r0, None))

    @pl.when(phase("v"))
    def _():
        store_p(lambda r, r0: r)

    @pl.when(jnp.logical_or(phase("merge_a"), phase("merge_b")))
    def _():
        store_p(lambda r, r0: _sigmoid(r))


def _in_proj(hn, w_all, layer, cos2, sin2, tm, widths, dk):
    m, k = hn.shape
    wc = PROJ_COLS
    ntm = cos2.shape[0] // tm
    n_glu = widths["glu_a"] // wc
    assert widths["glu_a"] == widths["glu_b"]
    bounds = {"glu": (0, n_glu)}
    step = n_glu
    for name in ("conv_gate", "q", "k", "v", "ret_gate", "merge_a", "merge_b"):
        n = widths[name] // (2 * wc)
        bounds[name] = (step, step + n)
        step += n
    n_p = step - n_glu
    blk0 = 2 * n_glu
    wa_map = lambda i, j: (layer, 0, jnp.where(j < n_glu, j, blk0 + 2 * (j - n_glu)))
    wb_map = lambda i, j: (layer, 0, jnp.where(j < n_glu, n_glu + j, blk0 + 2 * (j - n_glu) + 1))
    tab = pl.BlockSpec((tm, dk), lambda i, j: (i % ntm, 0))
    return pl.pallas_call(
        functools.partial(_in_proj_kernel, bounds=bounds, dk=dk, scale=dk ** -0.5),
        out_shape=(jax.ShapeDtypeStruct((m, n_glu * wc), BF16),
                   jax.ShapeDtypeStruct((m, n_p * 2 * wc), BF16)),
        grid=(m // tm, step),
        in_specs=[pl.BlockSpec((tm, k), lambda i, j: (i, 0)),
                  pl.BlockSpec((None, k, wc), wa_map),
                  pl.BlockSpec((None, k, wc), wb_map),
                  tab, tab],
        out_specs=(pl.BlockSpec((tm, wc), lambda i, j: (i, jnp.minimum(j, n_glu - 1))),
                   pl.BlockSpec((tm, 2 * wc), lambda i, j: (i, jnp.maximum(j - n_glu, 0)))),
        compiler_params=_params(("arbitrary", "arbitrary")),
        name="in_proj",
    )(hn, w_all, w_all, cos2, sin2)


def _conformer_kernel(u_ref, sgate_ref, sma_ref, cw_ref, cb_ref, lnw_ref, lnb_ref,
                      wproj_ref, o_ref, uext_ref, conv_ref, *, kw):
    t = pl.program_id(1)
    tt = u_ref.shape[0]
    d = u_ref.shape[1]
    ncb = d // LANES
    first = CONV_HIST - (kw - 1)

    @pl.when(t == 0)
    def _():
        uext_ref[:, 0:CONV_HIST, :] = jnp.zeros((ncb, CONV_HIST, LANES), F32)

    @pl.when(t > 0)
    def _():
        uext_ref[:, 0:CONV_HIST, :] = uext_ref[:, tt:tt + CONV_HIST, :]

    for c in range(ncb):
        uext_ref[c, CONV_HIST:CONV_HIST + tt, :] = u_ref[:, c * LANES:(c + 1) * LANES].astype(F32)

    def conv_block(c, carry):
        for rb in range(tt // CONV_ROW_BLOCK):
            r0 = rb * CONV_ROW_BLOCK
            acc = jnp.broadcast_to(cb_ref[c], (CONV_ROW_BLOCK, LANES))
            for k in range(kw):
                acc = acc + cw_ref[c, k:k + 1, :] * uext_ref[c, r0 + first + k:r0 + first + k + CONV_ROW_BLOCK, :]
            conv_ref[c, r0:r0 + CONV_ROW_BLOCK, :] = acc
        return carry

    lax.fori_loop(0, ncb, conv_block, 0)

    cv = conv_ref[...]
    mu = jnp.sum(jnp.sum(cv, axis=0), axis=-1, keepdims=True) / d
    dev = cv - mu[None]
    var = jnp.sum(jnp.sum(dev * dev, axis=0), axis=-1, keepdims=True) / d
    y = dev * lax.rsqrt(var + EPS)[None] * lnw_ref[...] + lnb_ref[...]
    act = _silu(y)
    lhs = jnp.concatenate(
        [(act[c] * sgate_ref[:, c * LANES:(c + 1) * LANES].astype(F32)).astype(BF16) for c in range(ncb)],
        axis=1)
    ya = jnp.dot(lhs, wproj_ref[...], preferred_element_type=F32)
    o_ref[...] = sma_ref[...].astype(F32) * ya


def _conformer(u, p, conv_w, conv_b, ln_w, ln_b, w_proj, nb, nt, pcol):
    m, d = u.shape
    kw = conv_w.shape[0]
    ncb = d // LANES
    tt = ROW_TILE
    cw = conv_w.reshape(kw, ncb, LANES).transpose(1, 0, 2)
    blk = lambda col: pl.BlockSpec((tt, d), lambda b, t, col=col: (b * nt + t, col))
    vec3 = pl.BlockSpec((ncb, 1, LANES), lambda b, t: (0, 0, 0))
    return pl.pallas_call(
        functools.partial(_conformer_kernel, kw=kw),
        out_shape=jax.ShapeDtypeStruct((m, d), F32),
        grid=(nb, nt),
        in_specs=[blk(0), blk(pcol["conv_gate"] // d), blk(pcol["merge_a"] // d),
                  pl.BlockSpec((ncb, kw, LANES), lambda b, t: (0, 0, 0)),
                  vec3, vec3, vec3,
                  pl.BlockSpec((d, d), lambda b, t: (0, 0), pipeline_mode=pl.Buffered(1))],
        out_specs=pl.BlockSpec((tt, d), lambda b, t: (b * nt + t, 0)),
        scratch_shapes=[pltpu.VMEM((ncb, CONV_HIST + tt, LANES), F32),
                        pltpu.VMEM((ncb, tt, LANES), F32)],
        compiler_params=_params(("parallel", "arbitrary")),
        name="conformer",
    )(u, p, p, cw, conv_b.reshape(ncb, 1, LANES), ln_w.reshape(ncb, 1, LANES),
      ln_b.reshape(ncb, 1, LANES), w_proj)


def _retention_kernel(logg_ref, q_ref, k_ref, v_ref, sg_ref, gnw_ref, gnb_ref,
                      o_ref, state_ref, decay_ref, xi_ref, zeta_ref, *, chunk):
    t = pl.program_id(0)
    b = pl.program_id(1)
    h = pl.program_id(2)
    nsub = q_ref.shape[0] // chunk
    lg = logg_ref[h]
    slot = b * pl.num_programs(2) + h

    @pl.when(jnp.logical_and(t == 0, b == 0))
    def _():
        i = lax.broadcasted_iota(jnp.int32, (chunk, chunk), 0)
        j = lax.broadcasted_iota(jnp.int32, (chunk, chunk), 1)
        dist = (i - j).astype(F32)
        decay_ref[h] = jnp.where(dist >= 0, jnp.exp(lg * jnp.maximum(dist, 0.0)), 0.0)
        idx = lax.broadcasted_iota(jnp.int32, (chunk, 1), 0).astype(F32)
        xi_ref[h] = jnp.exp(lg * (idx + 1.0))
        zeta_ref[h] = jnp.exp(lg * (chunk - 1.0 - idx))

    @pl.when(t == 0)
    def _():
        state_ref[slot] = jnp.zeros(state_ref.shape[1:], F32)

    g_chunk = jnp.exp(jnp.full((1, 1), lg * chunk, F32))

    def step(c, carry):
        rows = pl.ds(pl.multiple_of(c * chunk, chunk), chunk)
        qb = q_ref[rows, :]
        kb = k_ref[rows, :]
        v = v_ref[rows, :]
        state = state_ref[slot]
        s = lax.dot_general(qb, kb, (((1,), (1,)), ((), ())), preferred_element_type=F32) * decay_ref[h]
        inner = jnp.dot(s.astype(BF16), v, preferred_element_type=F32)
        cross = jnp.dot(qb, state.astype(BF16), preferred_element_type=F32) * xi_ref[h]
        kz = (kb.astype(F32) * zeta_ref[h]).astype(BF16)
        state_ref[slot] = state * g_chunk + lax.dot_general(
            kz, v, (((0,), (0,)), ((), ())), preferred_element_type=F32)
        o = inner + cross
        mu = jnp.mean(o, axis=-1, keepdims=True)
        dev = o - mu
        var = jnp.mean(dev * dev, axis=-1, keepdims=True)
        y = dev * lax.rsqrt(var + EPS) * gnw_ref[...] + gnb_ref[...]
        o_ref[rows, :] = (y * sg_ref[rows, :].astype(F32)).astype(o_ref.dtype)
        return carry

    lax.fori_loop(0, nsub, step, 0, unroll=3)


def _retention(p, log_g, gn_w, gn_b, nb, nt, tr, pcol, dk, dv):
    m = p.shape[0]
    heads = RET_HEADS
    chunk = ROW_TILE
    qblk = lambda col: pl.BlockSpec((tr, dk), lambda t, b, h, col=col: (b * nt + t, col + h))
    vblk = lambda col: pl.BlockSpec((tr, dv), lambda t, b, h, col=col: (b * nt + t, col + h))
    vec = pl.BlockSpec((1, dv), lambda t, b, h: (0, h))
    return pl.pallas_call(
        functools.partial(_retention_kernel, chunk=chunk),
        out_shape=jax.ShapeDtypeStruct((m, heads * dv), BF16),
        grid=(nt, nb, heads),
        in_specs=[pl.BlockSpec(memory_space=pltpu.SMEM),
                  qblk(pcol["q"] // dk), qblk(pcol["k"] // dk),
                  vblk(pcol["v"] // dv), vblk(pcol["ret_gate"] // dv),
                  vec, vec],
        out_specs=pl.BlockSpec((tr, dv), lambda t, b, h: (b * nt + t, h)),
        scratch_shapes=[pltpu.VMEM((nb * heads, dk, dv), F32),
                        pltpu.VMEM((heads, chunk, chunk), F32),
                        pltpu.VMEM((heads, chunk, 1), F32),
                        pltpu.VMEM((heads, chunk, 1), F32)],
        compiler_params=_params(("arbitrary", "arbitrary", "arbitrary")),
        name="retention",
    )(log_g, p, p, p, p, gn_w.reshape(1, -1), gn_b.reshape(1, -1))


def _tail_kernel(act_ref, wret_ref, smb_ref, ma_ref, wout_ref, g_ref, *refs, nt, pad_rows, from_x, last):
    i = pl.program_id(0)
    t = i % nt
    n_in = 2 if from_x else 1
    resid = _front_or_x(t, *refs[:n_in]) if from_x else refs[0][...]
    out_refs = refs[n_in:]
    tm = resid.shape[0]
    yb = jnp.dot(act_ref[...], wret_ref[...], preferred_element_type=F32)
    merged = (ma_ref[...] + smb_ref[...].astype(F32) * yb).astype(BF16)
    hn = resid + jnp.dot(merged, wout_ref[...], preferred_element_type=F32)
    row = t * tm + lax.broadcasted_iota(jnp.int32, (tm, 1), 0)
    hn = jnp.where(row >= pad_rows, hn, 0.0)
    y = hn * lax.rsqrt(jnp.mean(hn * hn, axis=-1, keepdims=True) + EPS)
    if not last:
        out_refs[0][...] = hn
    out_refs[-1][...] = (y * g_ref[...]).astype(out_refs[-1].dtype)


def _tail(act_b, w_ret, p, col_mb, m_a, w_out, g_next, resid, nb, nt, pad_rows, from_x, last, out_dtype):
    m, d = m_a.shape
    kdim = act_b.shape[1]
    tm = ROW_TILE
    row = pl.BlockSpec((tm, d), lambda i: (i, 0))
    whole = lambda r, c: pl.BlockSpec((r, c), lambda i: (0, 0), pipeline_mode=pl.Buffered(1))
    resid_specs = [pl.BlockSpec((tm, d), lambda i: (0, 0)), _x_tile_spec(tm, d, nt)] if from_x else [row]
    if last:
        out_shape = jax.ShapeDtypeStruct((nb, (nt - 1) * tm, d), out_dtype)
        out_specs = _x_tile_spec(tm, d, nt)
    else:
        out_shape = (jax.ShapeDtypeStruct((m, d), F32), jax.ShapeDtypeStruct((m, d), BF16))
        out_specs = (row, row)
    return pl.pallas_call(
        functools.partial(_tail_kernel, nt=nt, pad_rows=pad_rows, from_x=from_x, last=last),
        out_shape=out_shape,
        grid=(m // tm,),
        in_specs=[pl.BlockSpec((tm, kdim), lambda i: (i, 0)),
                  whole(kdim, d),
                  pl.BlockSpec((tm, d), lambda i: (i, col_mb)),
                  row,
                  whole(d, d),
                  pl.BlockSpec((1, d), lambda i: (0, 0))] + resid_specs,
        out_specs=out_specs,
        compiler_params=_params(("arbitrary",)),
        name="tail",
    )(act_b, w_ret, p, m_a, w_out, g_next.reshape(1, d), *resid)


def kernel(x, meta_tokens, norm_g, w_in, conv_w, conv_b, conv_ln_w, conv_ln_b, conv_out,
           ret_gn_w, ret_gn_b, ret_out, w_out, final_norm_g):
    nb, seq, d = x.shape
    depth = w_in.shape[0]
    n_meta = meta_tokens.shape[0]
    heads = RET_HEADS
    conv_width = conv_out.shape[1]
    ret_width = ret_out.shape[1]
    dk = d // heads
    dv = ret_width // heads
    qk_width = heads * dk
    assert seq % ROW_TILE == 0 and n_meta <= FRONT_ROWS and conv_width == d

    widths = {"glu_a": conv_width, "glu_b": conv_width, "conv_gate": conv_width, "q": qk_width, "k": qk_width,
              "v": ret_width, "ret_gate": ret_width, "merge_a": d, "merge_b": d}
    assert sum(widths.values()) == w_in.shape[2]
    pcol = {}
    off = 0
    for name in ("conv_gate", "q", "k", "v", "ret_gate", "merge_a", "merge_b"):
        pcol[name] = off
        off += widths[name]

    pad_rows = FRONT_ROWS - n_meta
    lp = FRONT_ROWS + seq
    nt = lp // ROW_TILE
    m = nb * lp

    front = jnp.concatenate([jnp.zeros((pad_rows, d), x.dtype), meta_tokens.astype(x.dtype)], axis=0)

    pos = (jnp.arange(lp, dtype=jnp.int32) - pad_rows).astype(F32)
    inv_freq = 1.0 / (ROPE_BASE ** jnp.linspace(0.0, 1.0, dk // 2, dtype=F32))
    ang2 = pos[:, None] * jnp.repeat(inv_freq, 2)[None]
    cos2 = jnp.cos(ang2)
    sin2 = jnp.sin(ang2) * jnp.tile(jnp.array([-1.0, 1.0], F32), dk // 2)[None]
    log_g = jnp.log(1.0 - jnp.exp2(-5.0 - jnp.arange(heads, dtype=F32)))

    tm_proj = 1280 if lp % 1280 == 0 else ROW_TILE
    tr = next(c for c in (3328, 1280, ROW_TILE) if lp % c == 0)

    hn = _first_rmsnorm(front, x, norm_g[0], nt)
    h = None
    for l in range(depth):
        u, p = _in_proj(hn, w_in, l, cos2, sin2, tm_proj, widths, dk)
        m_a = _conformer(u, p, conv_w[l], conv_b[l], conv_ln_w[l], conv_ln_b[l],
                         conv_out[l].astype(BF16), nb, nt, pcol)
        act_b = _retention(p, log_g, ret_gn_w[l], ret_gn_b[l], nb, lp // tr, tr, pcol, dk, dv)
        last = l == depth - 1
        g_next = final_norm_g if last else norm_g[l + 1]
        resid = (front, x) if l == 0 else (h,)
        res = _tail(act_b, ret_out[l].astype(BF16), p, pcol["merge_b"] // d, m_a, w_out[l].astype(BF16),
                    g_next, resid, nb, nt, pad_rows, from_x=(l == 0), last=last, out_dtype=x.dtype)
        if last:
            return res
        h, hn = res
```
